```python
import jax, jax.numpy as jnp
from jax import lax
import numpy as np

D_MODEL = 1024
BATCH = 8
SEQ = 2048
DEPTH = 2

CHUNK = 64
MIX_W = D_MODEL
MLSTM_HEADS = 4
MLSTM_DH = 3 * D_MODEL // 32
MLSTM_W = MLSTM_HEADS * MLSTM_DH
LRU_BLOCKS = 6
LRU_BW = D_MODEL // 16
LRU_W = LRU_BLOCKS * LRU_BW
LRU_CONV = 4
RG_C = 8.0
ATT_HEADS = 4
ATT_DH = D_MODEL // 16
ATT_W = ATT_HEADS * ATT_DH
IDX_HEADS = 4
IDX_DIM = 64
TOPK_MAX = 256
Q_BLOCK = CHUNK
D_FF = 11 * D_MODEL // 4
FFN_CONV = 3
EPS = 1e-6

IN_SPLITS = (MLSTM_W, MLSTM_W, MLSTM_W, MLSTM_W, MLSTM_HEADS, MLSTM_HEADS,
             LRU_W, LRU_W,
             ATT_W, ATT_W, ATT_W, IDX_HEADS * IDX_DIM, IDX_DIM, IDX_HEADS)
D_IN = sum(IN_SPLITS)

kernel_name = "hybrid_mlstm_rglru_dsa_block"


def rmsnorm(x, g):
    xf = x.astype(jnp.float32)
    y = xf * lax.rsqrt(jnp.mean(xf * xf, axis=-1, keepdims=True) + EPS)
    return (y * g.astype(jnp.float32)).astype(x.dtype)


def causal_depthwise_conv(x, w, b):
    width, c = w.shape
    xp = jnp.pad(x, ((0, 0), (width - 1, 0), (0, 0)))
    y = lax.conv_general_dilated(xp, w[:, None, :].astype(x.dtype), window_strides=(1,),
                                 padding='VALID', dimension_numbers=('NWC', 'WIO', 'NWC'),
                                 feature_group_count=c)
    return y + b.astype(x.dtype)


def split_cols(p):
    idx, acc = [], 0
    for s in IN_SPLITS[:-1]:
        acc += s
        idx.append(acc)
    return jnp.split(p, idx, axis=-1)


def mlstm_chunkwise(q, k, v, ig, fg):
    B, S, H, D = q.shape
    L = CHUNK
    nc = S // L
    f32 = jnp.float32
    q = q.astype(f32)
    k = k.astype(f32) * (D ** -0.5)
    v = v.astype(f32)
    log_f = jax.nn.log_sigmoid(fg.astype(f32))
    ig = ig.astype(f32)

    def to_chunks(a):
        return a.reshape(B, nc, L, H, D).transpose(1, 0, 3, 2, 4)

    def gate_chunks(a):
        return a.reshape(B, nc, L, H).transpose(1, 0, 3, 2)

    qs, ks, vs = to_chunks(q), to_chunks(k), to_chunks(v)
    i_s = gate_chunks(ig)
    b_s = jnp.cumsum(gate_chunks(log_f), axis=-1)
    tril = jnp.tril(jnp.ones((L, L), dtype=bool))

    def step(carry, inp):
        C, n, m = carry
        qc, kc, vc, ic, bc = inp
        d_log = bc[..., :, None] - bc[..., None, :] + ic[..., None, :]
        d_log = jnp.where(tril, d_log, -jnp.inf)
        inter = bc + m[..., None]
        m_t = jnp.maximum(inter, jnp.max(d_log, axis=-1))
        w_intra = jnp.exp(d_log - m_t[..., None])
        w_inter = jnp.exp(inter - m_t)
        s = jnp.einsum('bhtd,bhsd->bhts', qc, kc) * w_intra
        num = (w_inter[..., None] * jnp.einsum('bhtd,bhde->bhte', qc, C)
               + jnp.einsum('bhts,bhse->bhte', s, vc))
        den = w_inter * jnp.einsum('bhtd,bhd->bht', qc, n) + jnp.sum(s, axis=-1)
        h = num / jnp.maximum(jnp.abs(den), jnp.exp(-m_t))[..., None]
        b_last = bc[..., -1]
        g = b_last[..., None] - bc + ic
        m_new = jnp.maximum(b_last + m, jnp.max(g, axis=-1))
        wg = jnp.exp(g - m_new[..., None])
        decay = jnp.exp(b_last + m - m_new)
        C_new = decay[..., None, None] * C + jnp.einsum('bhs,bhsd,bhse->bhde', wg, kc, vc)
        n_new = decay[..., None] * n + jnp.einsum('bhs,bhsd->bhd', wg, kc)
        return (C_new, n_new, m_new), h

    init = (jnp.zeros((B, H, D, D), f32), jnp.zeros((B, H, D), f32), jnp.zeros((B, H), f32))
    _, hs = lax.scan(step, init, (qs, ks, vs, i_s, b_s))
    return hs.transpose(1, 0, 3, 2, 4).reshape(B, S, H, D)


def rglru(x, w_a, b_a, w_x, b_x, lam):
    B, S, C = x.shape
    f32 = jnp.float32
    xf = x.astype(f32)
    xb = xf.reshape(B, S, LRU_BLOCKS, LRU_BW)
    r = jax.nn.sigmoid(jnp.einsum('bsnc,ncd->bsnd', xb, w_a.astype(f32)).reshape(B, S, C)
                       + b_a.astype(f32))
    i = jax.nn.sigmoid(jnp.einsum('bsnc,ncd->bsnd', xb, w_x.astype(f32)).reshape(B, S, C)
                       + b_x.astype(f32))
    log_a = -RG_C * r * jax.nn.softplus(-lam.astype(f32))
    a = jnp.exp(log_a)
    u = jnp.sqrt(-jnp.expm1(2.0 * log_a)) * (i * xf)

    def combine(e1, e2):
        a1, b1 = e1
        a2, b2 = e2
        return a1 * a2, a2 * b1 + b2

    _, h = lax.associative_scan(combine, (a, u), axis=1)
    return h.astype(x.dtype)


def dsa_attention(q, k, v, iq, ik, iw):
    B, S, H, Dh = q.shape
    n_sel = min(TOPK_MAX, S // 4)
    nblk = S // Q_BLOCK
    key_chunk = jnp.arange(S) // CHUNK
    bidx = jnp.arange(B)[:, None, None]
    idx_scale = (IDX_HEADS ** -0.5) * (IDX_DIM ** -0.5)
    att_scale = Dh ** -0.5
    ikf = ik.astype(jnp.float32)

    def to_blocks(a):
        return a.reshape((B, nblk, Q_BLOCK) + a.shape[2:]).swapaxes(0, 1)

    def block(args):
        blk, qb, iqb, iwb = args
        t = blk * Q_BLOCK + jnp.arange(Q_BLOCK)
        admissible = key_chunk[None, :] <= (t // CHUNK)[:, None]
        rel = jax.nn.relu(jnp.einsum('bqhd,bsd->bqhs', iqb.astype(jnp.float32), ikf))
        score = jnp.einsum('bqh,bqhs->bqs', iwb.astype(jnp.float32), rel) * idx_scale
        score = jnp.where(admissible[None], score, -jnp.inf)
        top_val, top_idx = lax.top_k(score, n_sel)
        valid = jnp.isfinite(top_val)
        ks = k[bidx, top_idx]
        vs = v[bidx, top_idx]
        logits = jnp.einsum('bqhd,bqkhd->bqhk', qb, ks).astype(jnp.float32) * att_scale
        logits = jnp.where(valid[:, :, None, :], logits, -jnp.inf)
        p = jax.nn.softmax(logits, axis=-1)
        return jnp.einsum('bqhk,bqkhd->bqhd', p.astype(vs.dtype), vs)

    out = lax.map(block, (jnp.arange(nblk), to_blocks(q), to_blocks(iq), to_blocks(iw)))
    return out.swapaxes(0, 1).reshape(B, S, H, Dh)


def hybrid_mixer(xn, w_in, b_igate, b_fgate, mlstm_norm, lru_conv_w, lru_conv_b,
                 lru_w_a, lru_b_a, lru_w_x, lru_b_x, lru_lambda, w_out):
    B, S, _ = xn.shape
    p = xn @ w_in
    mq, mk, mv, mo, mi, mf, lx, lg, aq, ak, av, iq, ik, iw = split_cols(p)
    shp = (B, S, MLSTM_HEADS, MLSTM_DH)
    hA = mlstm_chunkwise(mq.reshape(shp), mk.reshape(shp), mv.reshape(shp),
                         mi + b_igate, mf + b_fgate).astype(xn.dtype)
    hA = rmsnorm(hA, mlstm_norm.reshape(MLSTM_HEADS, MLSTM_DH)).reshape(B, S, MLSTM_W)
    hA = hA * jax.nn.sigmoid(mo)
    xc = causal_depthwise_conv(lx, lru_conv_w, lru_conv_b)
    hB = rglru(xc, lru_w_a, lru_b_a, lru_w_x, lru_b_x, lru_lambda) * jax.nn.gelu(lg, approximate=True)
    ashp = (B, S, ATT_HEADS, ATT_DH)
    hC = dsa_attention(aq.reshape(ashp), ak.reshape(ashp), av.reshape(ashp),
                       iq.reshape(B, S, IDX_HEADS, IDX_DIM), ik, iw).reshape(B, S, ATT_W)
    return jnp.concatenate([hA, hB, hC], axis=-1) @ w_out


def conv_ffn(xn, ffn_up, ffn_conv_w, ffn_conv_b, ffn_down):
    u = causal_depthwise_conv(xn @ ffn_up, ffn_conv_w, ffn_conv_b)
    gate, up = jnp.split(u, 2, axis=-1)
    return (jax.nn.gelu(gate, approximate=True) * up) @ ffn_down


def setup_inputs(seed: int = 0) -> dict:
    key = jax.random.key(seed)
    ks = jax.random.split(key, 24)
    f32 = jnp.float32
    nrm = lambda k, shape, s: jax.random.normal(k, shape, f32) * s
    gain = lambda k, shape: 1.0 + 0.05 * jax.random.normal(k, shape, f32)
    u = jax.random.uniform(ks[20], (DEPTH, LRU_W), f32, minval=0.9, maxval=0.999)
    s = u ** (1.0 / RG_C)
    lru_lambda = jnp.log(s) - jnp.log1p(-s)
    return {
        "x": nrm(ks[0], (BATCH, SEQ, D_MODEL), 1.0),
        "norm_mix_pre": gain(ks[1], (DEPTH, D_MODEL)),
        "norm_mix_post": gain(ks[2], (DEPTH, D_MODEL)),
        "norm_ffn_pre": gain(ks[3], (DEPTH, D_MODEL)),
        "norm_ffn_post": gain(ks[4], (DEPTH, D_MODEL)),
        "w_in": nrm(ks[5], (DEPTH, D_MODEL, D_IN), D_MODEL ** -0.5),
        "b_igate": nrm(ks[6], (DEPTH, MLSTM_HEADS), 0.1),
        "b_fgate": 3.0 + nrm(ks[7], (DEPTH, MLSTM_HEADS), 0.1),
        "mlstm_norm": gain(ks[8], (DEPTH, MLSTM_W)),
        "lru_conv_w": nrm(ks[9], (DEPTH, LRU_CONV, LRU_W), LRU_CONV ** -0.5),
        "lru_conv_b": nrm(ks[10], (DEPTH, LRU_W), 0.01),
        "lru_w_a": nrm(ks[11], (DEPTH, LRU_BLOCKS, LRU_BW, LRU_BW), LRU_BW ** -0.5),
        "lru_b_a": nrm(ks[12], (DEPTH, LRU_W), 0.01),
        "lru_w_x": nrm(ks[13], (DEPTH, LRU_BLOCKS, LRU_BW, LRU_BW), LRU_BW ** -0.5),
        "lru_b_x": nrm(ks[14], (DEPTH, LRU_W), 0.01),
        "lru_lambda": lru_lambda,
        "w_out": nrm(ks[15], (DEPTH, MIX_W, D_MODEL), MIX_W ** -0.5),
        "ffn_up": nrm(ks[16], (DEPTH, D_MODEL, 2 * D_FF), D_MODEL ** -0.5),
        "ffn_conv_w": nrm(ks[17], (DEPTH, FFN_CONV, 2 * D_FF), FFN_CONV ** -0.5),
        "ffn_conv_b": nrm(ks[18], (DEPTH, 2 * D_FF), 0.01),
        "ffn_down": nrm(ks[19], (DEPTH, D_FF, D_MODEL), D_FF ** -0.5),
    }


def reference(x, norm_mix_pre, norm_mix_post, norm_ffn_pre, norm_ffn_post, w_in,
              b_igate, b_fgate, mlstm_norm, lru_conv_w, lru_conv_b, lru_w_a, lru_b_a,
              lru_w_x, lru_b_x, lru_lambda, w_out, ffn_up, ffn_conv_w, ffn_conv_b, ffn_down):
    for l in range(DEPTH):
        h = rmsnorm(x, norm_mix_pre[l])
        mix = hybrid_mixer(h, w_in[l], b_igate[l], b_fgate[l], mlstm_norm[l],
                           lru_conv_w[l], lru_conv_b[l], lru_w_a[l], lru_b_a[l],
                           lru_w_x[l], lru_b_x[l], lru_lambda[l], w_out[l])
        x = x + rmsnorm(mix, norm_mix_post[l])
        h = rmsnorm(x, norm_ffn_pre[l])
        x = x + rmsnorm(conv_ffn(h, ffn_up[l], ffn_conv_w[l], ffn_conv_b[l], ffn_down[l]),
                        norm_ffn_post[l])
    return x
```

```python
import functools

import jax
import jax.numpy as jnp
from jax import lax
from jax.experimental import pallas as pl
from jax.experimental.pallas import tpu as pltpu

F32 = jnp.float32
BF16 = jnp.bfloat16

LANES = 128
CHUNK = 64
MLSTM_HEADS = 4
MLSTM_DH = 96
HEAD_PAD = LANES
LRU_BLOCKS = 6
LRU_CONV = 4
RG_C = 8.0
ATT_HEADS = 4
ATT_DH = 64
IDX_HEADS = 4
IDX_DIM = 64
TOPK_MAX = 256
FFN_CONV = 3
EPS = 1e-6
VMEM_LIMIT = 56 * 1024 * 1024

NEG_INF = float("-inf")
INT_MIN = -2147483648


def _cparams(sem):
    return pltpu.CompilerParams(dimension_semantics=sem, vmem_limit_bytes=VMEM_LIMIT)


def _dot(a, b):
    return jnp.dot(a, b, preferred_element_type=F32)


def _dot_nt(a, b):
    return lax.dot_general(a, b, (((1,), (1,)), ((), ())), preferred_element_type=F32)


def _split_dot(x, ones_b):
    hi = x.astype(BF16)
    r1 = x - hi.astype(F32)
    mid = r1.astype(BF16)
    lo = (r1 - mid.astype(F32)).astype(BF16)
    return _dot(hi, ones_b) + _dot(mid, ones_b) + _dot(lo, ones_b)


def _log_sigmoid(x):
    return jnp.minimum(x, 0.0) - jnp.log1p(jnp.exp(-jnp.abs(x)))


def _sigmoid(x):
    return 1.0 / (1.0 + jnp.exp(-x))


def _expm1(x):
    series = x * (1.0 + x * (1 / 2) * (1.0 + x * (1 / 3) * (1.0 + x * (1 / 4) * (
        1.0 + x * (1 / 5) * (1.0 + x * (1 / 6) * (1.0 + x * (1 / 7)))))))
    return jnp.where(jnp.abs(x) < 0.35, series, jnp.exp(x) - 1.0)


def _gelu_tanh(x):
    c = 0.7978845608028654
    return 0.5 * x * (1.0 + jnp.tanh(c * (x + 0.044715 * (x * x * x))))


def _proj_body(x_ref, g_ref, w_ref, wt_ref, pa_ref, pl_ref, pc_ref, vt_ref, gt_ref,
               *, na, nl, nc):
    x = x_ref[0]
    ms = jnp.mean(x * x, axis=-1, keepdims=True)
    h = (x * lax.rsqrt(ms + EPS) * g_ref[...]).astype(BF16)
    step = 512
    for ref, base, width in ((pa_ref, 0, na), (pl_ref, na, nl), (pc_ref, na + nl, nc)):
        for lo in range(0, width, step):
            hi = min(lo + step, width)
            ref[0, :, lo:hi] = _dot(h, w_ref[:, base + lo:base + hi]).astype(ref.dtype)
    t = _dot_nt(wt_ref[...], h)
    vt_ref[0] = t[0:256].astype(BF16)
    gt_ref[0] = t[256:280]


def _proj(x, gain, w, wt, *, na, nl, nc, tm):
    B, S, D = x.shape
    nw = w.shape[1]
    grid = (B, S // tm)
    return pl.pallas_call(
        functools.partial(_proj_body, na=na, nl=nl, nc=nc),
        grid=grid,
        in_specs=[
            pl.BlockSpec((1, tm, D), lambda b, i: (b, i, 0)),
            pl.BlockSpec((1, D), lambda b, i: (0, 0)),
            pl.BlockSpec((D, nw), lambda b, i: (0, 0)),
            pl.BlockSpec((wt.shape[0], D), lambda b, i: (0, 0)),
        ],
        out_specs=[
            pl.BlockSpec((1, tm, na), lambda b, i: (b, i, 0)),
            pl.BlockSpec((1, tm, nl), lambda b, i: (b, i, 0)),
            pl.BlockSpec((1, tm, nc), lambda b, i: (b, i, 0)),
            pl.BlockSpec((1, 256, tm), lambda b, i: (b, 0, i)),
            pl.BlockSpec((1, 24, tm), lambda b, i: (b, 0, i)),
        ],
        out_shape=[
            jax.ShapeDtypeStruct((B, S, na), BF16),
            jax.ShapeDtypeStruct((B, S, nl), F32),
            jax.ShapeDtypeStruct((B, S, nc), BF16),
            jax.ShapeDtypeStruct((B, 256, S), BF16),
            jax.ShapeDtypeStruct((B, 24, S), F32),
        ],
        compiler_params=_cparams(("parallel", "parallel")),
        name="in_proj",
    )(x, gain, w, wt)


def _mlstm_body(pa_ref, gt_ref, bi_ref, bf_ref, gain_ref, out_ref, c_ref, *, n_pairs):
    H, P, L = MLSTM_HEADS, HEAD_PAD, CHUNK
    scale = MLSTM_DH ** -0.5
    c_ref[...] = jnp.zeros_like(c_ref)

    r128 = lax.broadcasted_iota(jnp.int32, (2 * L, 2 * L), 0)
    c128 = lax.broadcasted_iota(jnp.int32, (2 * L, 2 * L), 1)
    triu_bd = jnp.where((r128 // L == c128 // L) & (r128 <= c128), 1.0, 0.0).astype(BF16)
    r64 = lax.broadcasted_iota(jnp.int32, (L, L), 0)
    c64 = lax.broadcasted_iota(jnp.int32, (L, L), 1)
    tril = c64 <= r64
    ones_b = jnp.ones((L, P), BF16)
    lane_p = lax.broadcasted_iota(jnp.int32, (L, P), 1)
    is_den = lane_p == MLSTM_DH
    is_feat = lane_p < MLSTM_DH
    is_last = lax.broadcasted_iota(jnp.int32, (1, L), 1) == L - 1

    def pair_step(j, m_rows):
        col0 = pl.multiple_of(j * (2 * L), 2 * L)
        gi = gt_ref[0, 0:8, pl.ds(col0, 2 * L)] + bi_ref[...]
        lf = _log_sigmoid(gt_ref[0, 8:16, pl.ds(col0, 2 * L)] + bf_ref[...])
        bcum = _split_dot(lf, triu_bd)
        m_rows = list(m_rows)
        for half in range(2):
            row0 = pl.multiple_of(j * (2 * L) + half * L, L)
            lsl = slice(half * L, (half + 1) * L)
            gi_h, bcum_h, lf_h = gi[:, lsl], bcum[:, lsl], lf[:, lsl]
            for h in range(H):
                hs = slice(h * P, (h + 1) * P)
                irow = gi_h[h:h + 1]
                brow = bcum_h[h:h + 1]
                lrow = lf_h[h:h + 1]
                m_prev = m_rows[h]
                bc = _split_dot(jnp.where(tril, jnp.broadcast_to(lrow, (L, L)), 0.0), ones_b)
                bc64 = bc[:, 0:L]
                d_log = jnp.where(tril, bc64 - brow + irow, NEG_INF)
                inter = bc + m_prev
                m_t = jnp.maximum(inter, jnp.max(d_log, axis=1, keepdims=True))
                w_intra = jnp.exp(d_log - m_t[:, 0:L])
                w_inter = jnp.exp(inter - m_t)

                q = pa_ref[0, pl.ds(row0, L), h * P:(h + 1) * P]
                k = pa_ref[0, pl.ds(row0, L), (H + h) * P:(H + h + 1) * P]
                v = pa_ref[0, pl.ds(row0, L), (2 * H + h) * P:(2 * H + h + 1) * P]
                o = pa_ref[0, pl.ds(row0, L), (3 * H + h) * P:(3 * H + h + 1) * P]
                v_aug = jnp.where(is_den, 1.0, v.astype(F32))
                kT = k.astype(F32).T

                s = _dot(q, kT.astype(BF16)) * (w_intra * scale)
                c_prev = c_ref[h]
                num = w_inter * _dot(q, c_prev.astype(BF16)) + _dot(s.astype(BF16), v_aug.astype(BF16))
                den = jnp.sum(jnp.where(is_den, num, 0.0), axis=1, keepdims=True)
                hval = num / jnp.maximum(jnp.abs(den), jnp.exp(-m_t))
                hval = jnp.where(is_feat, hval, 0.0)
                var = jnp.sum(hval * hval, axis=1, keepdims=True) * (1.0 / MLSTM_DH)
                y = hval * lax.rsqrt(var + EPS) * gain_ref[:, hs]
                y = y * _sigmoid(o.astype(F32))
                out_ref[0, pl.ds(row0, L), hs] = y.astype(BF16)

                b_last = jnp.sum(jnp.where(is_last, brow, 0.0), axis=1, keepdims=True)
                grow = b_last - brow + irow
                m_new = jnp.maximum(b_last + m_prev, jnp.max(grow, axis=1, keepdims=True))
                wg = jnp.exp(grow - m_new) * scale
                decay = jnp.exp(b_last + m_prev - m_new)
                c_ref[h] = decay * c_prev + _dot((kT * wg).astype(BF16), v_aug.astype(BF16))
                m_rows[h] = m_new
        return tuple(m_rows)

    m0 = tuple(jnp.zeros((1, 1), F32) for _ in range(H))
    lax.fori_loop(0, n_pairs, pair_step, m0)


def _mlstm(pa, gt, bi, bf, gain):
    B, S, _ = pa.shape
    H, P = MLSTM_HEADS, HEAD_PAD
    n_pairs = S // (2 * CHUNK)
    return pl.pallas_call(
        functools.partial(_mlstm_body, n_pairs=n_pairs),
        grid=(B,),
        in_specs=[
            pl.BlockSpec((1, S, 4 * H * P), lambda b: (b, 0, 0)),
            pl.BlockSpec((1, 24, S), lambda b: (b, 0, 0)),
            pl.BlockSpec((8, 1), lambda b: (0, 0)),
            pl.BlockSpec((8, 1), lambda b: (0, 0)),
            pl.BlockSpec((1, H * P), lambda b: (0, 0)),
        ],
        out_specs=pl.BlockSpec((1, S, H * P), lambda b: (b, 0, 0)),
        out_shape=jax.ShapeDtypeStruct((B, S, H * P), BF16),
        scratch_shapes=[pltpu.VMEM((H, P, P), F32)],
        compiler_params=_cparams(("parallel",)),
        name="mlstm",
    )(pa, gt, bi, bf, gain)


def _lru_body(pl_ref, cw_ref, cb_ref, wa_ref, ba_ref, wx_ref, bx_ref, lam_ref, out_ref,
              *, n_tiles, tr, cw):
    C = cw
    neg_lam = -lam_ref[...]
    sp = jnp.maximum(neg_lam, 0.0) + jnp.log1p(jnp.exp(-jnp.abs(neg_lam)))
    row = lax.broadcasted_iota(jnp.int32, (tr, C), 0)
    sub = row & 7

    def tile_step(t, h_last):
        r0 = pl.multiple_of(t * tr, tr)
        x = pl_ref[0, pl.ds(r0, tr), 0:C]
        lg = pl_ref[0, pl.ds(r0, tr), C:2 * C]
        rp = pl.multiple_of(jnp.maximum(r0 - 8, 0), 8)
        xp = pl_ref[0, pl.ds(rp, 8), 0:C] * (t > 0).astype(F32)
        cat = jnp.concatenate([xp, x], axis=0)
        xc = x * cw_ref[LRU_CONV - 1:LRU_CONV, :] + cb_ref[...]
        for d in range(1, LRU_CONV):
            xs = pltpu.roll(cat, d, axis=0)[8:8 + tr]
            xc = xc + xs * cw_ref[LRU_CONV - 1 - d:LRU_CONV - d, :]
        xcb = xc.astype(BF16)
        r = _sigmoid(_dot(xcb, wa_ref[...]) + ba_ref[...])
        i = _sigmoid(_dot(xcb, wx_ref[...]) + bx_ref[...])
        log_a = -RG_C * r * sp
        a = jnp.exp(log_a)
        u = jnp.sqrt(-_expm1(2.0 * log_a)) * (i * xc)
        for d in (1, 2, 4):
            a_s = pltpu.roll(a, d, axis=0)
            u_s = pltpu.roll(u, d, axis=0)
            ok = sub >= d
            u = jnp.where(ok, u + a * u_s, u)
            a = jnp.where(ok, a * a_s, a)
        outs = []
        carry = h_last
        for g in range(tr // 8):
            hg = u[g * 8:(g + 1) * 8] + a[g * 8:(g + 1) * 8] * carry
            carry = hg[7:8]
            outs.append(hg)
        hfull = jnp.concatenate(outs, axis=0)
        out_ref[0, pl.ds(r0, tr), :] = (hfull * _gelu_tanh(lg)).astype(BF16)
        return carry

    lax.fori_loop(0, n_tiles, tile_step, jnp.zeros((1, C), F32))


def _lru(plx, cw, cb, wa, ba, wx, bx, lam):
    B, S, two_c = plx.shape
    C = two_c // 2
    tr = 64
    full = lambda shape: pl.BlockSpec(shape, lambda b: (0,) * len(shape))
    return pl.pallas_call(
        functools.partial(_lru_body, n_tiles=S // tr, tr=tr, cw=C),
        grid=(B,),
        in_specs=[
            pl.BlockSpec((1, S, two_c), lambda b: (b, 0, 0)),
            full((LRU_CONV, C)), full((1, C)),
            full((C, C)), full((1, C)), full((C, C)), full((1, C)), full((1, C)),
        ],
        out_specs=pl.BlockSpec((1, S, C), lambda b: (b, 0, 0)),
        out_shape=jax.ShapeDtypeStruct((B, S, C), BF16),
        compiler_params=_cparams(("parallel",)),
        name="rglru",
    )(plx, cw, cb, wa, ba, wx, bx, lam)


def _head_pair_block(t):
    lane = lax.broadcasted_iota(jnp.int32, t.shape, 1)
    zero = jnp.zeros_like(t)
    return jnp.concatenate([jnp.where(lane < 64, t, zero), jnp.where(lane >= 64, t, zero)], axis=0)


def _dsa_body(pc_ref, vt_ref, gt_ref, out_ref, key_ref, bias_ref, *, nk, n_sel, pair_base, idx_bits):
    QT = 2 * CHUNK
    p = pair_base + pl.program_id(1)
    q0 = pl.multiple_of(p * QT, QT)
    idx_scale = (IDX_HEADS ** -0.5) * (IDX_DIM ** -0.5)
    att_scale = ATT_DH ** -0.5

    aq_t = pc_ref[0, pl.ds(q0, QT), 0:256]
    iq_t = pc_ref[0, pl.ds(q0, QT), 512:768]
    iw = gt_ref[0, 16:24, pl.ds(q0, QT)] * idx_scale

    ik2 = pc_ref[0, 0:nk, 768:896]
    score = jnp.zeros((nk, QT), F32)
    for hp in range(IDX_HEADS // 2):
        blk = _head_pair_block(iq_t[:, hp * 128:(hp + 1) * 128])
        r = _dot_nt(ik2, blk)
        score = score + jnp.maximum(r[:, 0:QT], 0.0) * iw[2 * hp:2 * hp + 1, :]
        score = score + jnp.maximum(r[:, QT:2 * QT], 0.0) * iw[2 * hp + 1:2 * hp + 2, :]

    krow = lax.broadcasted_iota(jnp.int32, (nk, QT), 0)
    qlane = lax.broadcasted_iota(jnp.int32, (1, QT), 1)
    lim = ((q0 + qlane) // CHUNK + 1) * CHUNK
    adm = krow < lim
    score = jnp.where(adm, score + 0.0, NEG_INF)
    bits = pltpu.bitcast(score, jnp.int32)
    key_ref[...] = jnp.where(bits >= 0, bits, bits ^ jnp.int32(0x7FFFFFFF))

    def bit_step(i, ts):
        bit = lax.shift_left(jnp.int32(1), 31 - i)
        cand = ts ^ bit
        cnt = jnp.sum(jnp.where(key_ref[...] >= cand, 1.0, 0.0), axis=0, keepdims=True)
        return jnp.where(cnt >= n_sel, cand, ts)

    tau = lax.fori_loop(0, 32, bit_step, jnp.full((1, QT), INT_MIN, jnp.int32))
    keys = key_ref[...]
    cnt_gt = jnp.sum(jnp.where(keys > tau, 1.0, 0.0), axis=0, keepdims=True)
    need = n_sel - cnt_gt

    def idx_step(i, lo):
        cand = lo | lax.shift_left(jnp.int32(1), idx_bits - 1 - i)
        rr = lax.broadcasted_iota(jnp.int32, (nk, QT), 0)
        hit = jnp.where(key_ref[...] == tau, jnp.where(rr < cand, 1.0, 0.0), 0.0)
        cnt = jnp.sum(hit, axis=0, keepdims=True)
        return jnp.where(cnt < need, cand, lo)

    last = lax.fori_loop(0, idx_bits, idx_step, jnp.zeros((1, QT), jnp.int32))
    tie_bias = jnp.where(keys == tau, jnp.where(krow <= last, 0.0, NEG_INF), NEG_INF)
    bias_ref[...] = jnp.where(adm, jnp.where(keys > tau, 0.0, tie_bias), NEG_INF)

    outs = []
    for hp in range(ATT_HEADS // 2):
        q2 = aq_t[:, hp * 128:(hp + 1) * 128] * jnp.asarray(att_scale, BF16)
        blk = _head_pair_block(q2)
        lg = _dot_nt(pc_ref[0, 0:nk, 256 + hp * 128:256 + (hp + 1) * 128], blk)
        for sub in range(2):
            h = 2 * hp + sub
            l = lg[:, sub * QT:(sub + 1) * QT] + bias_ref[...]
            m = jnp.max(l, axis=0, keepdims=True)
            e = jnp.exp(l - m)
            den = jnp.sum(e, axis=0, keepdims=True)
            o_t = _dot(vt_ref[0, h * ATT_DH:(h + 1) * ATT_DH, 0:nk], e.astype(BF16))
            outs.append(o_t / den)
    out_t = jnp.concatenate(outs, axis=0)
    out_ref[0] = out_t.T.astype(BF16)


def _dsa_class(pc, vt, gt, *, nk, n_sel, pair_base, n_pairs):
    B, S, ncols = pc.shape
    QT = 2 * CHUNK
    idx_bits = max(1, (nk - 1).bit_length())
    return pl.pallas_call(
        functools.partial(_dsa_body, nk=nk, n_sel=n_sel, pair_base=pair_base, idx_bits=idx_bits),
        grid=(B, n_pairs),
        in_specs=[
            pl.BlockSpec((1, S, ncols), lambda b, i: (b, 0, 0)),
            pl.BlockSpec((1, 256, S), lambda b, i: (b, 0, 0)),
            pl.BlockSpec((1, 24, S), lambda b, i: (b, 0, 0)),
        ],
        out_specs=pl.BlockSpec((1, QT, 256), lambda b, i: (b, i, 0)),
        out_shape=jax.ShapeDtypeStruct((B, n_pairs * QT, 256), BF16),
        scratch_shapes=[pltpu.VMEM((nk, QT), jnp.int32), pltpu.VMEM((nk, QT), F32)],
        compiler_params=_cparams(("parallel", "parallel")),
        name=f"dsa_{nk}",
    )(pc, vt, gt)


def _dsa(pc, vt, gt, n_classes=4):
    B, S, _ = pc.shape
    QT = 2 * CHUNK
    n_pairs = S // QT
    n_sel = min(TOPK_MAX, S // 4)
    n_classes = min(n_classes, n_pairs)
    per = n_pairs // n_classes
    outs = []
    for c in range(n_classes):
        nk = (c + 1) * per * QT
        outs.append(_dsa_class(pc, vt, gt, nk=nk, n_sel=n_sel, pair_base=c * per, n_pairs=per))
    return jnp.concatenate(outs, axis=1)


def _outproj_body(x_ref, ha_ref, hb_ref, hc_ref, wa_ref, wb_ref, wc_ref, g_ref, o_ref):
    mix = _dot(ha_ref[0], wa_ref[...]) + _dot(hb_ref[0], wb_ref[...]) + _dot(hc_ref[0], wc_ref[...])
    ms = jnp.mean(mix * mix, axis=-1, keepdims=True)
    o_ref[0] = x_ref[0] + mix * lax.rsqrt(ms + EPS) * g_ref[...]


def _outproj(x, ha, hb, hc, wa, wb, wc, gain, *, tm):
    B, S, D = x.shape
    row = lambda n: pl.BlockSpec((1, tm, n), lambda b, i: (b, i, 0))
    full = lambda a: pl.BlockSpec(a.shape, lambda b, i: (0, 0))
    return pl.pallas_call(
        _outproj_body,
        grid=(B, S // tm),
        in_specs=[row(D), row(ha.shape[2]), row(hb.shape[2]), row(hc.shape[2]),
                  full(wa), full(wb), full(wc), full(gain)],
        out_specs=row(D),
        out_shape=jax.ShapeDtypeStruct((B, S, D), F32),
        compiler_params=_cparams(("parallel", "parallel")),
        name="out_proj",
    )(x, ha, hb, hc, wa, wb, wc, gain)


def _ffn_body(x_ref, gpre_ref, wg_ref, wu_ref, cwg_ref, cwu_ref, cbg_ref, cbu_ref, wd_ref, gpost_ref,
              o_ref, h_ref, acc_ref, halo_ref, *, n_j):
    i = pl.program_id(1)
    j = pl.program_id(2)

    @pl.when(j == 0)
    def _():
        x = x_ref[0]
        ms = jnp.mean(x * x, axis=-1, keepdims=True)
        h_ref[...] = (x * lax.rsqrt(ms + EPS) * gpre_ref[...]).astype(BF16)
        acc_ref[...] = jnp.zeros_like(acc_ref)

    @pl.when(i == 0)
    def _():
        halo_ref[j] = jnp.zeros(halo_ref.shape[1:], F32)

    h = h_ref[...]
    tm = h.shape[0]
    row = lax.broadcasted_iota(jnp.int32, (tm, wg_ref.shape[1]), 0)

    def conv(w_ref, cw_ref, cb_ref, slot):
        xu = _dot(h, w_ref[...])
        prev = halo_ref[j, slot]
        halo_ref[j, slot] = xu[tm - 8:tm]
        y = xu * cw_ref[0, FFN_CONV - 1:FFN_CONV, :] + cb_ref[0]
        for d in range(1, FFN_CONV):
            xs = pltpu.roll(xu, d, axis=0)
            for r in range(d):
                xs = jnp.where(row == r, prev[8 - d + r:8 - d + r + 1, :], xs)
            y = y + xs * cw_ref[0, FFN_CONV - 1 - d:FFN_CONV - d, :]
        return y

    gate = conv(wg_ref, cwg_ref, cbg_ref, 0)
    up = conv(wu_ref, cwu_ref, cbu_ref, 1)
    act = (_gelu_tanh(gate) * up).astype(BF16)
    acc_ref[...] += _dot(act, wd_ref[...])

    @pl.when(j == n_j - 1)
    def _():
        y = acc_ref[...]
        ms = jnp.mean(y * y, axis=-1, keepdims=True)
        o_ref[0] = x_ref[0] + y * lax.rsqrt(ms + EPS) * gpost_ref[...]


def _ffn(x, gpre, w_up, cw, cb, w_down, gpost, *, tm, tn):
    B, S, D = x.shape
    dff = w_down.shape[0]
    n_j = dff // tn
    return pl.pallas_call(
        functools.partial(_ffn_body, n_j=n_j),
        grid=(B, S // tm, n_j),
        in_specs=[
            pl.BlockSpec((1, tm, D), lambda b, i, j: (b, i, 0)),
            pl.BlockSpec((1, D), lambda b, i, j: (0, 0)),
            pl.BlockSpec((D, tn), lambda b, i, j: (0, j)),
            pl.BlockSpec((D, tn), lambda b, i, j: (0, j + n_j)),
            pl.BlockSpec((1, FFN_CONV, tn), lambda b, i, j: (j, 0, 0)),
            pl.BlockSpec((1, FFN_CONV, tn), lambda b, i, j: (j + n_j, 0, 0)),
            pl.BlockSpec((1, 1, tn), lambda b, i, j: (j, 0, 0)),
            pl.BlockSpec((1, 1, tn), lambda b, i, j: (j + n_j, 0, 0)),
            pl.BlockSpec((tn, D), lambda b, i, j: (j, 0)),
            pl.BlockSpec((1, D), lambda b, i, j: (0, 0)),
        ],
        out_specs=pl.BlockSpec((1, tm, D), lambda b, i, j: (b, i, 0)),
        out_shape=jax.ShapeDtypeStruct((B, S, D), F32),
        scratch_shapes=[
            pltpu.VMEM((tm, D), BF16),
            pltpu.VMEM((tm, D), F32),
            pltpu.VMEM((n_j, 2, 8, tn), F32),
        ],
        compiler_params=_cparams(("parallel", "arbitrary", "arbitrary")),
        name="conv_ffn",
    )(x, gpre, w_up, w_up, cw, cw, cb, cb, w_down, gpost)


def _pad_heads(w, n_heads, dh, pad):
    lead = w.shape[:-1]
    w = w.reshape(lead + (n_heads, dh))
    w = jnp.pad(w, [(0, 0)] * len(lead) + [(0, 0), (0, pad - dh)])
    return w.reshape(lead + (n_heads * pad,))


def _block_diag(w):
    nb, bw, _ = w.shape
    eye = jnp.eye(nb, dtype=w.dtype)
    return jnp.einsum("ncd,nm->ncmd", w, eye).reshape(nb * bw, nb * bw)


def _project(x, p):
    B, S, D = x.shape
    H, DH, P = MLSTM_HEADS, MLSTM_DH, HEAD_PAD
    mw = H * DH
    lw = p["lru_lambda"].shape[-1]
    aw = ATT_HEADS * ATT_DH
    w_in = p["w_in"]
    offs = [0]
    for s in (mw, mw, mw, mw, H, H, lw, lw, aw, aw, aw, IDX_HEADS * IDX_DIM, IDX_DIM, IDX_HEADS):
        offs.append(offs[-1] + s)
    col = lambda k: w_in[:, offs[k]:offs[k + 1]]
    mq, mk, mv, mo, mi, mf, lx, lg, aq, ak, av, iq, ik, iw = [col(k) for k in range(14)]

    w_a = jnp.concatenate([_pad_heads(t, H, DH, P) for t in (mq, mk, mv, mo)], axis=1)
    w_l = jnp.concatenate([lx, lg], axis=1)
    w_c = jnp.concatenate([aq, ak, iq, ik, ik], axis=1)
    w_main = jnp.concatenate([w_a, w_l, w_c], axis=1).astype(BF16)
    na, nl, nc = w_a.shape[1], w_l.shape[1], w_c.shape[1]
    pad4 = lambda t: jnp.pad(t, ((0, 0), (0, 4)))
    w_t = jnp.concatenate([av, pad4(mi), pad4(mf), pad4(iw)], axis=1).T.astype(BF16)

    return _proj(x, p["norm_mix_pre"].reshape(1, D), w_main, w_t, na=na, nl=nl, nc=nc, tm=min(512, S))


def _layer_parts(x, p):
    H, DH, P = MLSTM_HEADS, MLSTM_DH, HEAD_PAD
    mw = H * DH
    lw = p["lru_lambda"].shape[-1]
    pa, plx, pc, vt, gt = _project(x, p)

    pad_col = lambda b: jnp.pad(b.reshape(H, 1), ((0, 8 - H), (0, 0)))
    ha = _mlstm(pa, gt, pad_col(p["b_igate"]), pad_col(p["b_fgate"]),
                _pad_heads(p["mlstm_norm"].reshape(1, mw), H, DH, P))

    hb = _lru(plx, p["lru_conv_w"], p["lru_conv_b"].reshape(1, lw),
              _block_diag(p["lru_w_a"]).astype(BF16), p["lru_b_a"].reshape(1, lw),
              _block_diag(p["lru_w_x"]).astype(BF16), p["lru_b_x"].reshape(1, lw),
              p["lru_lambda"].reshape(1, lw))

    hc = _dsa(pc, vt, gt)
    return ha, hb, hc


def _layer(x, p):
    B, S, D = x.shape
    H, DH, P = MLSTM_HEADS, MLSTM_DH, HEAD_PAD
    mw = H * DH
    lw = p["lru_lambda"].shape[-1]
    ha, hb, hc = _layer_parts(x, p)

    w_out = p["w_out"]
    wo_a = _pad_heads(w_out[0:mw].T, H, DH, P).T.astype(BF16)
    wo_b = w_out[mw:mw + lw].astype(BF16)
    wo_c = w_out[mw + lw:].astype(BF16)
    x = _outproj(x, ha, hb, hc, wo_a, wo_b, wo_c, p["norm_mix_post"].reshape(1, D), tm=min(512, S))

    dff2 = p["ffn_up"].shape[1]
    tn = 256
    cw = p["ffn_conv_w"].reshape(FFN_CONV, dff2 // tn, tn).transpose(1, 0, 2)
    cb = p["ffn_conv_b"].reshape(dff2 // tn, 1, tn)
    x = _ffn(x, p["norm_ffn_pre"].reshape(1, D), p["ffn_up"].astype(BF16), cw, cb,
             p["ffn_down"].astype(BF16), p["norm_ffn_post"].reshape(1, D), tm=min(1024, S), tn=tn)
    return x


def kernel(x, norm_mix_pre, norm_mix_post, norm_ffn_pre, norm_ffn_post, w_in, b_igate, b_fgate,
           mlstm_norm, lru_conv_w, lru_conv_b, lru_w_a, lru_b_a, lru_w_x, lru_b_x, lru_lambda,
           w_out, ffn_up, ffn_conv_w, ffn_conv_b, ffn_down):
    params = dict(norm_mix_pre=norm_mix_pre, norm_mix_post=norm_mix_post, norm_ffn_pre=norm_ffn_pre,
                  norm_ffn_post=norm_ffn_post, w_in=w_in, b_igate=b_igate, b_fgate=b_fgate,
                  mlstm_norm=mlstm_norm, lru_conv_w=lru_conv_w, lru_conv_b=lru_conv_b,
                  lru_w_a=lru_w_a, lru_b_a=lru_b_a, lru_w_x=lru_w_x, lru_b_x=lru_b_x,
                  lru_lambda=lru_lambda, w_out=w_out, ffn_up=ffn_up, ffn_conv_w=ffn_conv_w,
                  ffn_conv_b=ffn_conv_b, ffn_down=ffn_down)
    for l in range(w_in.shape[0]):
        x = _layer(x, {k: v[l] for k, v in params.items()})
    return x
```

```python
import functools

import jax
import jax.numpy as jnp
from jax import lax
from jax.experimental import pallas as pl
from jax.experimental.pallas import tpu as pltpu

F32 = jnp.float32
BF16 = jnp.bfloat16

LANES = 128
CHUNK = 64
MLSTM_HEADS = 4
MLSTM_DH = 96
HEAD_PAD = LANES
LRU_BLOCKS = 6
LRU_CONV = 4
RG_C = 8.0
ATT_HEADS = 4
ATT_DH = 64
IDX_HEADS = 4
IDX_DIM = 64
TOPK_MAX = 256
FFN_CONV = 3
EPS = 1e-6
VMEM_LIMIT = 56 * 1024 * 1024

NEG_INF = float("-inf")
INT_MIN = -2147483648
KEY_NEG_INF = -2139095041


def _cparams(sem):
    return pltpu.CompilerParams(dimension_semantics=sem, vmem_limit_bytes=VMEM_LIMIT)


def _dot(a, b):
    return jnp.dot(a, b, preferred_element_type=F32)


def _dot_nt(a, b):
    return lax.dot_general(a, b, (((1,), (1,)), ((), ())), preferred_element_type=F32)


def _split_dot(x, ones_b):
    hi = x.astype(BF16)
    r1 = x - hi.astype(F32)
    mid = r1.astype(BF16)
    lo = (r1 - mid.astype(F32)).astype(BF16)
    return _dot(hi, ones_b) + _dot(mid, ones_b) + _dot(lo, ones_b)


def _log_sigmoid(x):
    return jnp.minimum(x, 0.0) - jnp.log1p(jnp.exp(-jnp.abs(x)))


def _sigmoid(x):
    return 1.0 / (1.0 + jnp.exp(-x))


def _expm1(x):
    series = x * (1.0 + x * (1 / 2) * (1.0 + x * (1 / 3) * (1.0 + x * (1 / 4) * (
        1.0 + x * (1 / 5) * (1.0 + x * (1 / 6) * (1.0 + x * (1 / 7)))))))
    return jnp.where(jnp.abs(x) < 0.35, series, jnp.exp(x) - 1.0)


def _gelu_tanh(x):
    c = 0.7978845608028654
    return 0.5 * x * (1.0 + jnp.tanh(c * (x + 0.044715 * (x * x * x))))


def _proj_body(x_ref, g_ref, w_ref, wt_ref, pa_ref, pl_ref, pc_ref, vt_ref, gt_ref,
               *, na, nl, nc):
    x = x_ref[0]
    ms = jnp.mean(x * x, axis=-1, keepdims=True)
    h = (x * lax.rsqrt(ms + EPS) * g_ref[...]).astype(BF16)
    step = 512
    for ref, base, width in ((pa_ref, 0, na), (pl_ref, na, nl), (pc_ref, na + nl, nc)):
        for lo in range(0, width, step):
            hi = min(lo + step, width)
            ref[0, :, lo:hi] = _dot(h, w_ref[:, base + lo:base + hi]).astype(ref.dtype)
    t = _dot_nt(wt_ref[...], h)
    vt_ref[0] = t[0:256].astype(BF16)
    gt_ref[0] = t[256:280]


def _proj(x, gain, w, wt, *, na, nl, nc, tm):
    B, S, D = x.shape
    nw = w.shape[1]
    grid = (B, S // tm)
    return pl.pallas_call(
        functools.partial(_proj_body, na=na, nl=nl, nc=nc),
        grid=grid,
        in_specs=[
            pl.BlockSpec((1, tm, D), lambda b, i: (b, i, 0)),
            pl.BlockSpec((1, D), lambda b, i: (0, 0)),
            pl.BlockSpec((D, nw), lambda b, i: (0, 0)),
            pl.BlockSpec((wt.shape[0], D), lambda b, i: (0, 0)),
        ],
        out_specs=[
            pl.BlockSpec((1, tm, na), lambda b, i: (b, i, 0)),
            pl.BlockSpec((1, tm, nl), lambda b, i: (b, i, 0)),
            pl.BlockSpec((1, tm, nc), lambda b, i: (b, i, 0)),
            pl.BlockSpec((1, 256, tm), lambda b, i: (b, 0, i)),
            pl.BlockSpec((1, 24, tm), lambda b, i: (b, 0, i)),
        ],
        out_shape=[
            jax.ShapeDtypeStruct((B, S, na), BF16),
            jax.ShapeDtypeStruct((B, S, nl), F32),
            jax.ShapeDtypeStruct((B, S, nc), BF16),
            jax.ShapeDtypeStruct((B, 256, S), BF16),
            jax.ShapeDtypeStruct((B, 24, S), F32),
        ],
        compiler_params=_cparams(("parallel", "parallel")),
        name="in_proj",
    )(x, gain, w, wt)


def _mlstm_body(pa_ref, gt_ref, bi_ref, bf_ref, gain_ref, out_ref, c_ref, *, n_pairs):
    H, P, L = MLSTM_HEADS, HEAD_PAD, CHUNK
    scale = MLSTM_DH ** -0.5
    c_ref[...] = jnp.zeros_like(c_ref)

    r128 = lax.broadcasted_iota(jnp.int32, (2 * L, 2 * L), 0)
    c128 = lax.broadcasted_iota(jnp.int32, (2 * L, 2 * L), 1)
    triu_bd = jnp.where((r128 // L == c128 // L) & (r128 <= c128), 1.0, 0.0).astype(BF16)
    r64 = lax.broadcasted_iota(jnp.int32, (L, L), 0)
    c64 = lax.broadcasted_iota(jnp.int32, (L, L), 1)
    tril = c64 <= r64
    ones_b = jnp.ones((L, P), BF16)
    lane_p = lax.broadcasted_iota(jnp.int32, (L, P), 1)
    is_den = lane_p == MLSTM_DH
    is_feat = lane_p < MLSTM_DH
    is_last = lax.broadcasted_iota(jnp.int32, (1, L), 1) == L - 1

    def pair_step(j, m_rows):
        col0 = pl.multiple_of(j * (2 * L), 2 * L)
        gi = gt_ref[0, 0:8, pl.ds(col0, 2 * L)] + bi_ref[...]
        lf = _log_sigmoid(gt_ref[0, 8:16, pl.ds(col0, 2 * L)] + bf_ref[...])
        bcum = _split_dot(lf, triu_bd)
        m_rows = list(m_rows)
        for half in range(2):
            row0 = pl.multiple_of(j * (2 * L) + half * L, L)
            lsl = slice(half * L, (half + 1) * L)
            gi_h, bcum_h, lf_h = gi[:, lsl], bcum[:, lsl], lf[:, lsl]
            for h in range(H):
                hs = slice(h * P, (h + 1) * P)
                irow = gi_h[h:h + 1]
                brow = bcum_h[h:h + 1]
                lrow = lf_h[h:h + 1]
                m_prev = m_rows[h]
                bc = _split_dot(jnp.where(tril, jnp.broadcast_to(lrow, (L, L)), 0.0), ones_b)
                bc64 = bc[:, 0:L]
                d_log = jnp.where(tril, bc64 - brow + irow, NEG_INF)
                inter = bc + m_prev
                m_t = jnp.maximum(inter, jnp.max(d_log, axis=1, keepdims=True))
                w_intra = jnp.exp(d_log - m_t[:, 0:L])
                w_inter = jnp.exp(inter - m_t)

                q = pa_ref[0, pl.ds(row0, L), h * P:(h + 1) * P]
                k = pa_ref[0, pl.ds(row0, L), (H + h) * P:(H + h + 1) * P]
                v = pa_ref[0, pl.ds(row0, L), (2 * H + h) * P:(2 * H + h + 1) * P]
                o = pa_ref[0, pl.ds(row0, L), (3 * H + h) * P:(3 * H + h + 1) * P]
                v_aug = jnp.where(is_den, 1.0, v.astype(F32))
                kT = k.astype(F32).T

                s = _dot(q, kT.astype(BF16)) * (w_intra * scale)
                c_prev = c_ref[h]
                num = w_inter * _dot(q, c_prev.astype(BF16)) + _dot(s.astype(BF16), v_aug.astype(BF16))
                den = jnp.sum(jnp.where(is_den, num, 0.0), axis=1, keepdims=True)
                hval = num / jnp.maximum(jnp.abs(den), jnp.exp(-m_t))
                hval = jnp.where(is_feat, hval, 0.0)
                var = jnp.sum(hval * hval, axis=1, keepdims=True) * (1.0 / MLSTM_DH)
                y = hval * lax.rsqrt(var + EPS) * gain_ref[:, hs]
                y = y * _sigmoid(o.astype(F32))
                out_ref[0, pl.ds(row0, L), hs] = y.astype(BF16)

                b_last = jnp.sum(jnp.where(is_last, brow, 0.0), axis=1, keepdims=True)
                grow = b_last - brow + irow
                m_new = jnp.maximum(b_last + m_prev, jnp.max(grow, axis=1, keepdims=True))
                wg = jnp.exp(grow - m_new) * scale
                decay = jnp.exp(b_last + m_prev - m_new)
                c_ref[h] = decay * c_prev + _dot((kT * wg).astype(BF16), v_aug.astype(BF16))
                m_rows[h] = m_new
        return tuple(m_rows)

    m0 = tuple(jnp.zeros((1, 1), F32) for _ in range(H))
    lax.fori_loop(0, n_pairs, pair_step, m0)


def _mlstm(pa, gt, bi, bf, gain):
    B, S, _ = pa.shape
    H, P = MLSTM_HEADS, HEAD_PAD
    n_pairs = S // (2 * CHUNK)
    return pl.pallas_call(
        functools.partial(_mlstm_body, n_pairs=n_pairs),
        grid=(B,),
        in_specs=[
            pl.BlockSpec((1, S, 4 * H * P), lambda b: (b, 0, 0)),
            pl.BlockSpec((1, 24, S), lambda b: (b, 0, 0)),
            pl.BlockSpec((8, 1), lambda b: (0, 0)),
            pl.BlockSpec((8, 1), lambda b: (0, 0)),
            pl.BlockSpec((1, H * P), lambda b: (0, 0)),
        ],
        out_specs=pl.BlockSpec((1, S, H * P), lambda b: (b, 0, 0)),
        out_shape=jax.ShapeDtypeStruct((B, S, H * P), BF16),
        scratch_shapes=[pltpu.VMEM((H, P, P), F32)],
        compiler_params=_cparams(("parallel",)),
        name="mlstm",
    )(pa, gt, bi, bf, gain)


def _lru_body(pl_ref, cw_ref, cb_ref, wa_ref, ba_ref, wx_ref, bx_ref, lam_ref, out_ref,
              *, n_tiles, tr, cw):
    C = cw
    neg_lam = -lam_ref[...]
    sp = jnp.maximum(neg_lam, 0.0) + jnp.log1p(jnp.exp(-jnp.abs(neg_lam)))
    row = lax.broadcasted_iota(jnp.int32, (tr, C), 0)
    sub = row & 7

    def tile_step(t, h_last):
        r0 = pl.multiple_of(t * tr, tr)
        x = pl_ref[0, pl.ds(r0, tr), 0:C]
        lg = pl_ref[0, pl.ds(r0, tr), C:2 * C]
        rp = pl.multiple_of(jnp.maximum(r0 - 8, 0), 8)
        xp = pl_ref[0, pl.ds(rp, 8), 0:C] * (t > 0).astype(F32)
        cat = jnp.concatenate([xp, x], axis=0)
        xc = x * cw_ref[LRU_CONV - 1:LRU_CONV, :] + cb_ref[...]
        for d in range(1, LRU_CONV):
            xs = pltpu.roll(cat, d, axis=0)[8:8 + tr]
            xc = xc + xs * cw_ref[LRU_CONV - 1 - d:LRU_CONV - d, :]
        xcb = xc.astype(BF16)
        r = _sigmoid(_dot(xcb, wa_ref[...]) + ba_ref[...])
        i = _sigmoid(_dot(xcb, wx_ref[...]) + bx_ref[...])
        log_a = -RG_C * r * sp
        a = jnp.exp(log_a)
        u = jnp.sqrt(-_expm1(2.0 * log_a)) * (i * xc)
        for d in (1, 2, 4):
            a_s = pltpu.roll(a, d, axis=0)
            u_s = pltpu.roll(u, d, axis=0)
            ok = sub >= d
            u = jnp.where(ok, u + a * u_s, u)
            a = jnp.where(ok, a * a_s, a)
        outs = []
        carry = h_last
        for g in range(tr // 8):
            hg = u[g * 8:(g + 1) * 8] + a[g * 8:(g + 1) * 8] * carry
            carry = hg[7:8]
            outs.append(hg)
        hfull = jnp.concatenate(outs, axis=0)
        out_ref[0, pl.ds(r0, tr), :] = (hfull * _gelu_tanh(lg)).astype(BF16)
        return carry

    lax.fori_loop(0, n_tiles, tile_step, jnp.zeros((1, C), F32))


def _lru(plx, cw, cb, wa, ba, wx, bx, lam):
    B, S, two_c = plx.shape
    C = two_c // 2
    tr = 64
    full = lambda shape: pl.BlockSpec(shape, lambda b: (0,) * len(shape))
    return pl.pallas_call(
        functools.partial(_lru_body, n_tiles=S // tr, tr=tr, cw=C),
        grid=(B,),
        in_specs=[
            pl.BlockSpec((1, S, two_c), lambda b: (b, 0, 0)),
            full((LRU_CONV, C)), full((1, C)),
            full((C, C)), full((1, C)), full((C, C)), full((1, C)), full((1, C)),
        ],
        out_specs=pl.BlockSpec((1, S, C), lambda b: (b, 0, 0)),
        out_shape=jax.ShapeDtypeStruct((B, S, C), BF16),
        compiler_params=_cparams(("parallel",)),
        name="rglru",
    )(plx, cw, cb, wa, ba, wx, bx, lam)


def _col_reduce(x, op):
    n, w = x.shape
    g = 64 if n % 64 == 0 else 8
    part = op(x.reshape(n // g, g, w), axis=0)
    return op(part, axis=0, keepdims=True)


def _head_pair_block(t):
    lane = lax.broadcasted_iota(jnp.int32, t.shape, 1)
    zero = jnp.zeros_like(t)
    return jnp.concatenate([jnp.where(lane < 64, t, zero), jnp.where(lane >= 64, t, zero)], axis=0)


DSA_SLAB = 256
DSA_CNT = 128


def _dsa_body(pc_ref, vt_ref, gt_ref, out_ref, key_ref, hi_ref, lo_ref, bias_ref, l_ref, e_ref,
              *, nk, n_sel, pair_base, idx_bits):
    QT = 2 * CHUNK
    p = pair_base + pl.program_id(1)
    q0 = pl.multiple_of(p * QT, QT)
    idx_scale = (IDX_HEADS ** -0.5) * (IDX_DIM ** -0.5)
    att_scale = ATT_DH ** -0.5

    aq_t = pc_ref[0, pl.ds(q0, QT), 0:256]
    iq_t = pc_ref[0, pl.ds(q0, QT), 512:768]
    iw = gt_ref[0, 16:24, pl.ds(q0, QT)] * idx_scale
    qlane = lax.broadcasted_iota(jnp.int32, (1, QT), 1)
    lim = ((q0 + qlane) // CHUNK + 1) * CHUNK
    slabs = [(r0, min(DSA_SLAB, nk - r0)) for r0 in range(0, nk, DSA_SLAB)]

    iblk = [_head_pair_block(iq_t[:, hp * 128:(hp + 1) * 128]) for hp in range(IDX_HEADS // 2)]
    for r0, n in slabs:
        ik2 = pc_ref[0, r0:r0 + n, 768:896]
        score = jnp.zeros((n, QT), F32)
        for hp in range(IDX_HEADS // 2):
            r = _dot_nt(ik2, iblk[hp])
            score = score + jnp.maximum(r[:, 0:QT], 0.0) * iw[2 * hp:2 * hp + 1, :]
            score = score + jnp.maximum(r[:, QT:2 * QT], 0.0) * iw[2 * hp + 1:2 * hp + 2, :]
        krow = r0 + lax.broadcasted_iota(jnp.int32, (n, QT), 0)
        score = jnp.where(krow < lim, score + 0.0, NEG_INF)
        bits = pltpu.bitcast(score, jnp.int32)
        key = jnp.where(bits >= 0, bits, bits ^ jnp.int32(0x7FFFFFFF))
        key_ref[r0:r0 + n, :] = key
        hi_ref[r0:r0 + n, :] = (key >> 16).astype(jnp.int16)
        lo_ref[r0:r0 + n, :] = ((key & 0xFFFF) - 32768).astype(jnp.int16)

    def count_ge(ref, cand):
        c16 = jnp.broadcast_to(cand, (DSA_CNT, QT)).astype(jnp.int16)
        acc = jnp.zeros((DSA_CNT, QT), jnp.int16)
        for r0 in range(0, nk, DSA_CNT):
            acc = acc + jnp.where(ref[r0:r0 + DSA_CNT, :] >= c16, jnp.int16(1), jnp.int16(0))
        return jnp.sum(acc.astype(F32), axis=0, keepdims=True)

    def make_step(ref, target):
        def step(i, carry):
            t, c_t = carry
            cand = t + lax.shift_left(jnp.int32(1), 15 - i)
            cnt = count_ge(ref, cand)
            ok = cnt >= target
            return jnp.where(ok, cand, t), jnp.where(ok, cnt, c_t)
        return step

    t_min = jnp.full((1, QT), -32768, jnp.int32)
    t_hi, _ = lax.fori_loop(0, 16, make_step(hi_ref, float(n_sel)), (t_min, jnp.full((1, QT), float(nk), F32)))
    cnt_hi_gt = count_ge(hi_ref, t_hi + 1)
    hi16 = jnp.broadcast_to(t_hi, (DSA_CNT, QT)).astype(jnp.int16)
    for r0 in range(0, nk, DSA_CNT):
        lo_ref[r0:r0 + DSA_CNT, :] = jnp.where(hi_ref[r0:r0 + DSA_CNT, :] == hi16,
                                               lo_ref[r0:r0 + DSA_CNT, :], jnp.int16(-32768))
    in_bucket = count_ge(hi_ref, t_hi) - cnt_hi_gt
    t_lo, c_lo = lax.fori_loop(0, 16, make_step(lo_ref, n_sel - cnt_hi_gt), (t_min, in_bucket))
    tau = t_hi * 65536 + (t_lo + 32768)
    cnt_ge = cnt_hi_gt + c_lo
    valid = tau > KEY_NEG_INF
    partial_ties = jnp.max(jnp.where(valid, cnt_ge - n_sel, 0.0)) > 0.0

    @pl.when(jnp.logical_not(partial_ties))
    def _():
        tau_c = jnp.maximum(tau, KEY_NEG_INF + 1)
        for r0, n in slabs:
            bias_ref[r0:r0 + n, :] = jnp.where(key_ref[r0:r0 + n, :] >= tau_c, 0.0, NEG_INF)

    @pl.when(partial_ties)
    def _():
        tau_t = jnp.maximum(tau, KEY_NEG_INF)
        keys = key_ref[...]
        need = n_sel - _col_reduce(jnp.where(keys > tau_t, 1.0, 0.0), jnp.sum)

        def idx_step(i, lo):
            cand = lo | lax.shift_left(jnp.int32(1), idx_bits - 1 - i)
            rr = lax.broadcasted_iota(jnp.int32, (nk, QT), 0)
            hit = jnp.where(key_ref[...] == tau_t, jnp.where(rr < cand, 1.0, 0.0), 0.0)
            return jnp.where(_col_reduce(hit, jnp.sum) < need, cand, lo)

        last = lax.fori_loop(0, idx_bits, idx_step, jnp.zeros((1, QT), jnp.int32))
        last = jnp.where(valid, last, -1)
        krow = lax.broadcasted_iota(jnp.int32, (nk, QT), 0)
        tie_bias = jnp.where(keys == tau_t, jnp.where(krow <= last, 0.0, NEG_INF), NEG_INF)
        bias_ref[...] = jnp.where(keys > tau_t, 0.0, tie_bias)

    outs = []
    for hp in range(ATT_HEADS // 2):
        q2 = aq_t[:, hp * 128:(hp + 1) * 128] * jnp.asarray(att_scale, BF16)
        blk = _head_pair_block(q2)
        mx = [jnp.full((64, QT), NEG_INF, F32) for _ in range(2)]
        for r0, n in slabs:
            lg = _dot_nt(pc_ref[0, r0:r0 + n, 256 + hp * 128:256 + (hp + 1) * 128], blk)
            b = bias_ref[r0:r0 + n, :]
            for sub in range(2):
                l = lg[:, sub * QT:(sub + 1) * QT] + b
                l_ref[sub, r0:r0 + n, :] = l
                mx[sub] = jnp.maximum(mx[sub], jnp.max(l.reshape(n // 64, 64, QT), axis=0))
        for sub in range(2):
            h = 2 * hp + sub
            m = jnp.max(mx[sub], axis=0, keepdims=True)
            den = jnp.zeros((64, QT), F32)
            for r0, n in slabs:
                e = jnp.exp(l_ref[sub, r0:r0 + n, :] - m)
                den = den + jnp.sum(e.reshape(n // 64, 64, QT), axis=0)
                e_ref[r0:r0 + n, :] = e.astype(BF16)
            o_t = _dot(vt_ref[0, h * ATT_DH:(h + 1) * ATT_DH, 0:nk], e_ref[...])
            outs.append(o_t / jnp.sum(den, axis=0, keepdims=True))
    out_t = jnp.concatenate(outs, axis=0)
    out_ref[0] = out_t.T.astype(BF16)


def _dsa_class(pc, vt, gt, *, nk, n_sel, pair_base, n_pairs):
    B, S, ncols = pc.shape
    QT = 2 * CHUNK
    idx_bits = max(1, (nk - 1).bit_length())
    return pl.pallas_call(
        functools.partial(_dsa_body, nk=nk, n_sel=n_sel, pair_base=pair_base, idx_bits=idx_bits),
        grid=(B, n_pairs),
        in_specs=[
            pl.BlockSpec((1, nk, ncols), lambda b, i: (b, 0, 0)),
            pl.BlockSpec((1, 256, nk), lambda b, i: (b, 0, 0)),
            pl.BlockSpec((1, 24, nk), lambda b, i: (b, 0, 0)),
        ],
        out_specs=pl.BlockSpec((1, QT, 256), lambda b, i: (b, i, 0)),
        out_shape=jax.ShapeDtypeStruct((B, n_pairs * QT, 256), BF16),
        scratch_shapes=[
            pltpu.VMEM((nk, QT), jnp.int32),
            pltpu.VMEM((nk, QT), jnp.int16),
            pltpu.VMEM((nk, QT), jnp.int16),
            pltpu.VMEM((nk, QT), F32),
            pltpu.VMEM((2, nk, QT), F32),
            pltpu.VMEM((nk, QT), BF16),
        ],
        compiler_params=_cparams(("parallel", "parallel")),
        name=f"dsa_{nk}",
    )(pc, vt, gt)


def _dsa(pc, vt, gt, n_classes=8):
    B, S, _ = pc.shape
    QT = 2 * CHUNK
    n_pairs = S // QT
    n_sel = min(TOPK_MAX, S // 4)
    n_classes = min(n_classes, n_pairs)
    per = n_pairs // n_classes
    outs = []
    for c in range(n_classes):
        nk = (c + 1) * per * QT
        outs.append(_dsa_class(pc, vt, gt, nk=nk, n_sel=n_sel, pair_base=c * per, n_pairs=per))
    return jnp.concatenate(outs, axis=1)


def _outproj_body(x_ref, ha_ref, hb_ref, hc_ref, wa_ref, wb_ref, wc_ref, g_ref, o_ref):
    mix = _dot(ha_ref[0], wa_ref[...]) + _dot(hb_ref[0], wb_ref[...]) + _dot(hc_ref[0], wc_ref[...])
    ms = jnp.mean(mix * mix, axis=-1, keepdims=True)
    o_ref[0] = x_ref[0] + mix * lax.rsqrt(ms + EPS) * g_ref[...]


def _outproj(x, ha, hb, hc, wa, wb, wc, gain, *, tm):
    B, S, D = x.shape
    row = lambda n: pl.BlockSpec((1, tm, n), lambda b, i: (b, i, 0))
    full = lambda a: pl.BlockSpec(a.shape, lambda b, i: (0, 0))
    return pl.pallas_call(
        _outproj_body,
        grid=(B, S // tm),
        in_specs=[row(D), row(ha.shape[2]), row(hb.shape[2]), row(hc.shape[2]),
                  full(wa), full(wb), full(wc), full(gain)],
        out_specs=row(D),
        out_shape=jax.ShapeDtypeStruct((B, S, D), F32),
        compiler_params=_cparams(("parallel", "parallel")),
        name="out_proj",
    )(x, ha, hb, hc, wa, wb, wc, gain)


def _ffn_body(x_ref, gpre_ref, wup_ref, cw_ref, cb_ref, wd_ref, gpost_ref, o_ref, h_ref, act_ref, halo_ref,
              *, tm, dff, tn, rb):
    i = pl.program_id(1)
    x = x_ref[0]
    ms = jnp.mean(x * x, axis=-1, keepdims=True)
    h_ref[...] = (x * lax.rsqrt(ms + EPS) * gpre_ref[...]).astype(BF16)

    @pl.when(i == 0)
    def _():
        halo_ref[...] = jnp.zeros_like(halo_ref)

    def row_block(r, carry):
        r0 = pl.multiple_of(r * rb, rb)
        hrows = h_ref[pl.ds(r0, rb), :]

        def conv(col0, slot, c0):
            xu = _dot(hrows, wup_ref[:, col0:col0 + tn])
            prev = halo_ref[slot, :, c0:c0 + tn]
            halo_ref[slot, :, c0:c0 + tn] = xu[rb - 8:rb]
            cat = jnp.concatenate([prev, xu], axis=0)
            y = xu * cw_ref[FFN_CONV - 1:FFN_CONV, col0:col0 + tn] + cb_ref[:, col0:col0 + tn]
            for d in range(1, FFN_CONV):
                y = y + pltpu.roll(cat, d, axis=0)[8:8 + rb] * cw_ref[FFN_CONV - 1 - d:FFN_CONV - d, col0:col0 + tn]
            return y

        for c0 in range(0, dff, tn):
            gate = conv(c0, 0, c0)
            up = conv(dff + c0, 1, c0)
            act_ref[:, c0:c0 + tn] = (_gelu_tanh(gate) * up).astype(BF16)
        y = _dot(act_ref[...], wd_ref[...])
        ms_y = jnp.mean(y * y, axis=-1, keepdims=True)
        o_ref[0, pl.ds(r0, rb), :] = x_ref[0, pl.ds(r0, rb), :] + y * lax.rsqrt(ms_y + EPS) * gpost_ref[...]
        return carry

    lax.fori_loop(0, tm // rb, row_block, 0)


def _ffn(x, gpre, w_up, cw, cb, w_down, gpost, *, tm, tn, rb):
    B, S, D = x.shape
    dff = w_down.shape[0]
    resident = lambda a: pl.BlockSpec(a.shape, lambda b, i: (0, 0), pipeline_mode=pl.Buffered(1))
    return pl.pallas_call(
        functools.partial(_ffn_body, tm=tm, dff=dff, tn=tn, rb=rb),
        grid=(B, S // tm),
        in_specs=[
            pl.BlockSpec((1, tm, D), lambda b, i: (b, i, 0)),
            resident(gpre), resident(w_up), resident(cw), resident(cb), resident(w_down), resident(gpost),
        ],
        out_specs=pl.BlockSpec((1, tm, D), lambda b, i: (b, i, 0)),
        out_shape=jax.ShapeDtypeStruct((B, S, D), F32),
        scratch_shapes=[
            pltpu.VMEM((tm, D), BF16),
            pltpu.VMEM((rb, dff), BF16),
            pltpu.VMEM((2, 8, dff), F32),
        ],
        compiler_params=_cparams(("parallel", "arbitrary")),
        name="conv_ffn",
    )(x, gpre, w_up, cw, cb, w_down, gpost)


def _pad_heads(w, n_heads, dh, pad):
    lead = w.shape[:-1]
    w = w.reshape(lead + (n_heads, dh))
    w = jnp.pad(w, [(0, 0)] * len(lead) + [(0, 0), (0, pad - dh)])
    return w.reshape(lead + (n_heads * pad,))


def _block_diag(w):
    nb, bw, _ = w.shape
    eye = jnp.eye(nb, dtype=w.dtype)
    return jnp.einsum("ncd,nm->ncmd", w, eye).reshape(nb * bw, nb * bw)


def _project(x, p):
    B, S, D = x.shape
    H, DH, P = MLSTM_HEADS, MLSTM_DH, HEAD_PAD
    mw = H * DH
    lw = p["lru_lambda"].shape[-1]
    aw = ATT_HEADS * ATT_DH
    w_in = p["w_in"]
    offs = [0]
    for s in (mw, mw, mw, mw, H, H, lw, lw, aw, aw, aw, IDX_HEADS * IDX_DIM, IDX_DIM, IDX_HEADS):
        offs.append(offs[-1] + s)
    col = lambda k: w_in[:, offs[k]:offs[k + 1]]
    mq, mk, mv, mo, mi, mf, lx, lg, aq, ak, av, iq, ik, iw = [col(k) for k in range(14)]

    w_a = jnp.concatenate([_pad_heads(t, H, DH, P) for t in (mq, mk, mv, mo)], axis=1)
    w_l = jnp.concatenate([lx, lg], axis=1)
    w_c = jnp.concatenate([aq, ak, iq, ik, ik], axis=1)
    w_main = jnp.concatenate([w_a, w_l, w_c], axis=1).astype(BF16)
    na, nl, nc = w_a.shape[1], w_l.shape[1], w_c.shape[1]
    pad4 = lambda t: jnp.pad(t, ((0, 0), (0, 4)))
    w_t = jnp.concatenate([av, pad4(mi), pad4(mf), pad4(iw)], axis=1).T.astype(BF16)

    return _proj(x, p["norm_mix_pre"].reshape(1, D), w_main, w_t, na=na, nl=nl, nc=nc, tm=min(512, S))


def _layer_parts(x, p):
    H, DH, P = MLSTM_HEADS, MLSTM_DH, HEAD_PAD
    mw = H * DH
    lw = p["lru_lambda"].shape[-1]
    pa, plx, pc, vt, gt = _project(x, p)

    pad_col = lambda b: jnp.pad(b.reshape(H, 1), ((0, 8 - H), (0, 0)))
    ha = _mlstm(pa, gt, pad_col(p["b_igate"]), pad_col(p["b_fgate"]),
                _pad_heads(p["mlstm_norm"].reshape(1, mw), H, DH, P))

    hb = _lru(plx, p["lru_conv_w"], p["lru_conv_b"].reshape(1, lw),
              _block_diag(p["lru_w_a"]).astype(BF16), p["lru_b_a"].reshape(1, lw),
              _block_diag(p["lru_w_x"]).astype(BF16), p["lru_b_x"].reshape(1, lw),
              p["lru_lambda"].reshape(1, lw))

    hc = _dsa(pc, vt, gt)
    return ha, hb, hc


def _layer(x, p):
    B, S, D = x.shape
    H, DH, P = MLSTM_HEADS, MLSTM_DH, HEAD_PAD
    mw = H * DH
    lw = p["lru_lambda"].shape[-1]
    ha, hb, hc = _layer_parts(x, p)

    w_out = p["w_out"]
    wo_a = _pad_heads(w_out[0:mw].T, H, DH, P).T.astype(BF16)
    wo_b = w_out[mw:mw + lw].astype(BF16)
    wo_c = w_out[mw + lw:].astype(BF16)
    x = _outproj(x, ha, hb, hc, wo_a, wo_b, wo_c, p["norm_mix_post"].reshape(1, D), tm=min(512, S))

    dff2 = p["ffn_up"].shape[1]
    x = _ffn(x, p["norm_ffn_pre"].reshape(1, D), p["ffn_up"].astype(BF16), p["ffn_conv_w"],
             p["ffn_conv_b"].reshape(1, dff2), p["ffn_down"].astype(BF16), p["norm_ffn_post"].reshape(1, D),
             tm=min(1024, S), tn=256, rb=256)
    return x


def kernel(x, norm_mix_pre, norm_mix_post, norm_ffn_pre, norm_ffn_post, w_in, b_igate, b_fgate,
           mlstm_norm, lru_conv_w, lru_conv_b, lru_w_a, lru_b_a, lru_w_x, lru_b_x, lru_lambda,
           w_out, ffn_up, ffn_conv_w, ffn_conv_b, ffn_down):
    params = dict(norm_mix_pre=norm_mix_pre, norm_mix_post=norm_mix_post, norm_ffn_pre=norm_ffn_pre,
                  norm_ffn_post=norm_ffn_post, w_in=w_in, b_igate=b_igate, b_fgate=b_fgate,
                  mlstm_norm=mlstm_norm, lru_conv_w=lru_conv_w, lru_conv_b=lru_conv_b,
                  lru_w_a=lru_w_a, lru_b_a=lru_b_a, lru_w_x=lru_w_x, lru_b_x=lru_b_x,
                  lru_lambda=lru_lambda, w_out=w_out, ffn_up=ffn_up, ffn_conv_w=ffn_conv_w,
                  ffn_conv_b=ffn_conv_b, ffn_down=ffn_down)
    for l in range(w_in.shape[0]):
        x = _layer(x, {k: v[l] for k, v in params.items()})
    return x
```

```python
import functools

import jax
import jax.numpy as jnp
from jax import lax
from jax.experimental import pallas as pl
from jax.experimental.pallas import tpu as pltpu

F32 = jnp.float32
BF16 = jnp.bfloat16

LANES = 128
CHUNK = 64
MLSTM_HEADS = 4
MLSTM_DH = 96
HEAD_PAD = LANES
LRU_BLOCKS = 6
LRU_CONV = 4
LRU_TILE = 64
RG_C = 8.0
ATT_HEADS = 4
ATT_DH = 64
IDX_HEADS = 4
IDX_DIM = 64
TOPK_MAX = 256
FFN_CONV = 3
EPS = 1e-6
VMEM_LIMIT = 56 * 1024 * 1024

NEG_INF = float("-inf")
INT_MIN = -2147483648
KEY_NEG_INF = -2139095040


def _cparams(sem):
    return pltpu.CompilerParams(dimension_semantics=sem, vmem_limit_bytes=VMEM_LIMIT)


def _dot(a, b):
    return jnp.dot(a, b, preferred_element_type=F32)


def _dot_nt(a, b):
    return lax.dot_general(a, b, (((1,), (1,)), ((), ())), preferred_element_type=F32)


def _split_dot(x, ones_b):
    hi = x.astype(BF16)
    r1 = x - hi.astype(F32)
    mid = r1.astype(BF16)
    lo = (r1 - mid.astype(F32)).astype(BF16)
    return _dot(hi, ones_b) + _dot(mid, ones_b) + _dot(lo, ones_b)


def _log_sigmoid(x):
    return jnp.minimum(x, 0.0) - jnp.log1p(jnp.exp(-jnp.abs(x)))


def _sigmoid(x):
    return 0.5 * jnp.tanh(0.5 * x) + 0.5


def _one_minus_sq_exp(log_a, a):
    x = 2.0 * log_a
    series = -x * (1.0 + x * (1 / 2 + x * (1 / 6 + x * (1 / 24 + x * (1 / 120)))))
    return jnp.where(x > -0.1, series, 1.0 - a * a)


def _gelu_tanh(x):
    c = 0.7978845608028654
    return 0.5 * x * (1.0 + jnp.tanh(c * (x + 0.044715 * (x * x * x))))


def _proj_body(x_ref, g_ref, w_ref, wt_ref, pa_ref, pl_ref, pc_ref, vt_ref, gt_ref,
               *, na, nl, nc):
    x = x_ref[0]
    ms = jnp.mean(x * x, axis=-1, keepdims=True)
    h = (x * lax.rsqrt(ms + EPS) * g_ref[...]).astype(BF16)
    step = 512
    for ref, base, width in ((pa_ref, 0, na), (pl_ref, na, nl), (pc_ref, na + nl, nc)):
        for lo in range(0, width, step):
            hi = min(lo + step, width)
            ref[0, :, lo:hi] = _dot(h, w_ref[:, base + lo:base + hi]).astype(ref.dtype)
    t = _dot_nt(wt_ref[...], h)
    vt_ref[0] = t[0:256].astype(BF16)
    gt_ref[0] = t[256:280]


def _proj(x, gain, w, wt, *, na, nl, nc, tm):
    B, S, D = x.shape
    nw = w.shape[1]
    grid = (B, S // tm)
    return pl.pallas_call(
        functools.partial(_proj_body, na=na, nl=nl, nc=nc),
        grid=grid,
        in_specs=[
            pl.BlockSpec((1, tm, D), lambda b, i: (b, i, 0)),
            pl.BlockSpec((1, D), lambda b, i: (0, 0)),
            pl.BlockSpec((D, nw), lambda b, i: (0, 0)),
            pl.BlockSpec((wt.shape[0], D), lambda b, i: (0, 0)),
        ],
        out_specs=[
            pl.BlockSpec((1, tm, na), lambda b, i: (b, i, 0)),
            pl.BlockSpec((1, tm, nl), lambda b, i: (b, i, 0)),
            pl.BlockSpec((1, tm, nc), lambda b, i: (b, i, 0)),
            pl.BlockSpec((1, 256, tm), lambda b, i: (b, 0, i)),
            pl.BlockSpec((1, 24, tm), lambda b, i: (b, 0, i)),
        ],
        out_shape=[
            jax.ShapeDtypeStruct((B, S, na), BF16),
            jax.ShapeDtypeStruct((B, S, nl), F32),
            jax.ShapeDtypeStruct((B, S, nc), BF16),
            jax.ShapeDtypeStruct((B, 256, S), BF16),
            jax.ShapeDtypeStruct((B, 24, S), F32),
        ],
        compiler_params=_cparams(("parallel", "parallel")),
        name="in_proj",
    )(x, gain, w, wt)


def _mixers_body(pa_ref, gt_ref, bi_ref, bf_ref, gain_ref,
                 pl_ref, cw_ref, cb_ref, wa_ref, ba_ref, wx_ref, bx_ref, lam_ref,
                 out_ref, outb_ref, c_ref, row_ref, bc_ref, rm_ref, mm_ref, *, n_pairs):
    H, P, L = MLSTM_HEADS, HEAD_PAD, CHUNK
    scale = MLSTM_DH ** -0.5
    c_ref[...] = jnp.zeros_like(c_ref)
    lru_refs = (pl_ref, cw_ref, cb_ref, wa_ref, ba_ref, wx_ref, bx_ref)
    sp = _lru_softplus_neg(lam_ref[...])

    r128 = lax.broadcasted_iota(jnp.int32, (2 * L, 2 * L), 0)
    c128 = lax.broadcasted_iota(jnp.int32, (2 * L, 2 * L), 1)
    triu_bd = jnp.where((r128 // L == c128 // L) & (r128 <= c128), 1.0, 0.0).astype(BF16)
    r64 = lax.broadcasted_iota(jnp.int32, (L, L), 0)
    c64 = lax.broadcasted_iota(jnp.int32, (L, L), 1)
    tril = c64 <= r64
    lane_p = lax.broadcasted_iota(jnp.int32, (L, P), 1)
    is_den = lane_p == MLSTM_DH
    is_feat = lane_p < MLSTM_DH
    is_last = lax.broadcasted_iota(jnp.int32, (8, L), 1) == L - 1

    def gate_step(j, carry):
        m_col, h_lru = carry
        h_lru = _lru_tile(lru_refs, outb_ref, sp, j, h_lru)
        col0 = pl.multiple_of(j * (2 * L), 2 * L)
        gi = gt_ref[0, 0:8, pl.ds(col0, 2 * L)] + bi_ref[...]
        lf = _log_sigmoid(gt_ref[0, 8:16, pl.ds(col0, 2 * L)] + bf_ref[...])
        bcum = _split_dot(lf, triu_bd)
        row_ref[0, :, pl.ds(col0, 2 * L)] = gi
        row_ref[1, :, pl.ds(col0, 2 * L)] = bcum
        for half in range(2):
            c = 2 * j + half
            row0 = pl.multiple_of(c * L, L)
            lsl = slice(half * L, (half + 1) * L)
            gi_h, bcum_h, lf_h = gi[:, lsl], bcum[:, lsl], lf[:, lsl]
            mm_ref[c] = jnp.broadcast_to(m_col, (8, P))
            b_last = jnp.sum(jnp.where(is_last, bcum_h, 0.0), axis=1, keepdims=True)
            gmax = jnp.max(b_last - bcum_h + gi_h, axis=1, keepdims=True)
            m_col = jnp.maximum(b_last + m_col, gmax)
            for h in range(H):
                lrow = jnp.broadcast_to(lf_h[h:h + 1], (L, L))
                bc = jnp.sum(jnp.where(tril, lrow, 0.0), axis=1, keepdims=True)
                d_log = jnp.where(tril, bc - bcum_h[h:h + 1] + gi_h[h:h + 1], NEG_INF)
                bc_ref[h, pl.ds(row0, L), :] = jnp.broadcast_to(bc, (L, P))
                rm_ref[h, pl.ds(row0, L), :] = jnp.broadcast_to(jnp.max(d_log, axis=1, keepdims=True), (L, P))
        return m_col, h_lru

    lru_c = cw_ref.shape[1]
    m_end, h_lru = lax.fori_loop(0, n_pairs, gate_step,
                                 (jnp.zeros((8, 1), F32), jnp.zeros((1, lru_c), F32)))
    mm_ref[2 * n_pairs] = jnp.broadcast_to(m_end, (8, P))

    def pair_step(j, h_lru):
        h_lru = _lru_tile(lru_refs, outb_ref, sp, n_pairs + j, h_lru)
        col0 = pl.multiple_of(j * (2 * L), 2 * L)
        gi = row_ref[0, :, pl.ds(col0, 2 * L)]
        bcum = row_ref[1, :, pl.ds(col0, 2 * L)]
        for half in range(2):
            c = 2 * j + half
            row0 = pl.multiple_of(c * L, L)
            lsl = slice(half * L, (half + 1) * L)
            gi_h, bcum_h = gi[:, lsl], bcum[:, lsl]
            m_prev_all = mm_ref[c]
            m_new_all = mm_ref[c + 1]
            for h in range(H):
                hs = slice(h * P, (h + 1) * P)
                irow = gi_h[h:h + 1]
                brow = bcum_h[h:h + 1]
                m_prev = m_prev_all[h:h + 1]
                m_new = m_new_all[h:h + 1]
                bc = bc_ref[h, pl.ds(row0, L), :]
                d_log = jnp.where(tril, bc[:, 0:L] - brow + irow, NEG_INF)
                inter = bc + m_prev
                m_t = jnp.maximum(inter, rm_ref[h, pl.ds(row0, L), :])
                w_intra = jnp.exp(d_log - m_t[:, 0:L])
                w_inter = jnp.exp(inter - m_t)

                q = pa_ref[0, pl.ds(row0, L), h * P:(h + 1) * P]
                k = pa_ref[0, pl.ds(row0, L), (H + h) * P:(H + h + 1) * P]
                v = pa_ref[0, pl.ds(row0, L), (2 * H + h) * P:(2 * H + h + 1) * P]
                o = pa_ref[0, pl.ds(row0, L), (3 * H + h) * P:(3 * H + h + 1) * P]
                v_aug = jnp.where(is_den, 1.0, v.astype(F32)).astype(BF16)
                kT = k.astype(F32).T

                s = _dot(q, kT.astype(BF16)) * (w_intra * scale)
                c_prev = c_ref[h]
                num = w_inter * _dot(q, c_prev.astype(BF16)) + _dot(s.astype(BF16), v_aug)
                den = jnp.sum(jnp.where(is_den, num, 0.0), axis=1, keepdims=True)
                feat = jnp.where(is_feat, num, 0.0)
                ssq = jnp.sum(feat * feat, axis=1, keepdims=True)
                dd = jnp.maximum(jnp.abs(den), jnp.exp(-m_t))
                inv = 1.0 / dd
                rs = inv * lax.rsqrt(ssq * (inv * inv) * (1.0 / MLSTM_DH) + EPS)
                y = feat * rs * gain_ref[:, hs] * _sigmoid(o.astype(F32))
                out_ref[0, pl.ds(row0, L), hs] = y.astype(BF16)

                b_last = bc[L - 1:L, :]
                grow = b_last[:, 0:L] - brow + irow
                wg = jnp.exp(grow - m_new[:, 0:L]) * scale
                decay = jnp.exp(b_last + m_prev - m_new)
                c_ref[h] = decay * c_prev + _dot((kT * wg).astype(BF16), v_aug)
        return h_lru

    lax.fori_loop(0, n_pairs, pair_step, h_lru)


def _mixers(pa, gt, bi, bf, gain, plx, cw, cb, wa, ba, wx, bx, lam):
    B, S, _ = pa.shape
    H, P = MLSTM_HEADS, HEAD_PAD
    C = lam.shape[1]
    n_pairs = S // (2 * CHUNK)
    assert S // LRU_TILE == 2 * n_pairs
    full = lambda a: pl.BlockSpec(a.shape, lambda b: (0,) * a.ndim)
    return pl.pallas_call(
        functools.partial(_mixers_body, n_pairs=n_pairs),
        grid=(B,),
        in_specs=[
            pl.BlockSpec((1, S, 4 * H * P), lambda b: (b, 0, 0)),
            pl.BlockSpec((1, 24, S), lambda b: (b, 0, 0)),
            full(bi), full(bf), full(gain),
            pl.BlockSpec((1, S, 2 * C), lambda b: (b, 0, 0)),
            full(cw), full(cb), full(wa), full(ba), full(wx), full(bx), full(lam),
        ],
        out_specs=[pl.BlockSpec((1, S, H * P), lambda b: (b, 0, 0)),
                   pl.BlockSpec((1, S, C), lambda b: (b, 0, 0))],
        out_shape=[jax.ShapeDtypeStruct((B, S, H * P), BF16),
                   jax.ShapeDtypeStruct((B, S, C), BF16)],
        scratch_shapes=[
            pltpu.VMEM((H, P, P), F32),
            pltpu.VMEM((2, 8, S), F32),
            pltpu.VMEM((H, S, P), F32),
            pltpu.VMEM((H, S, P), F32),
            pltpu.VMEM((2 * n_pairs + 1, 8, P), F32),
        ],
        compiler_params=_cparams(("parallel",)),
        name="mixers_ab",
    )(pa, gt, bi, bf, gain, plx, cw, cb, wa, ba, wx, bx, lam)


def _lru_softplus_neg(lam):
    neg_lam = -lam
    return jnp.maximum(neg_lam, 0.0) + jnp.log1p(jnp.exp(-jnp.abs(neg_lam)))


def _lru_tile(lru_refs, out_ref, sp, t, h_last):
    pl_ref, cw_ref, cb_ref, wa_ref, ba_ref, wx_ref, bx_ref = lru_refs
    tr = LRU_TILE
    C = cw_ref.shape[1]
    sub = lax.broadcasted_iota(jnp.int32, (tr, C), 0) & 7
    r0 = pl.multiple_of(t * tr, tr)
    x = pl_ref[0, pl.ds(r0, tr), 0:C]
    lg = pl_ref[0, pl.ds(r0, tr), C:2 * C]
    rp = pl.multiple_of(jnp.maximum(r0 - 8, 0), 8)
    xp = pl_ref[0, pl.ds(rp, 8), 0:C] * jnp.where(t > 0, 1.0, 0.0)
    cat = jnp.concatenate([xp, x], axis=0)
    xc = x * cw_ref[LRU_CONV - 1:LRU_CONV, :] + cb_ref[...]
    for d in range(1, LRU_CONV):
        xs = pltpu.roll(cat, d, axis=0)[8:8 + tr]
        xc = xc + xs * cw_ref[LRU_CONV - 1 - d:LRU_CONV - d, :]
    xcb = xc.astype(BF16)
    r = _sigmoid(_dot(xcb, wa_ref[...]) + ba_ref[...])
    i = _sigmoid(_dot(xcb, wx_ref[...]) + bx_ref[...])
    log_a = -RG_C * r * sp
    a = jnp.exp(log_a)
    u = jnp.sqrt(_one_minus_sq_exp(log_a, a)) * (i * xc)
    for d in (1, 2, 4):
        a_s = pltpu.roll(a, d, axis=0)
        u_s = pltpu.roll(u, d, axis=0)
        ok = sub >= d
        u = jnp.where(ok, u + a * u_s, u)
        a = jnp.where(ok, a * a_s, a)
    outs = []
    carry = h_last
    for g in range(tr // 8):
        hg = u[g * 8:(g + 1) * 8] + a[g * 8:(g + 1) * 8] * carry
        carry = hg[7:8]
        outs.append(hg)
    hfull = jnp.concatenate(outs, axis=0)
    out_ref[0, pl.ds(r0, tr), :] = (hfull * _gelu_tanh(lg)).astype(BF16)
    return carry


def _col_reduce(x, op):
    n, w = x.shape
    g = 64 if n % 64 == 0 else 8
    part = op(x.reshape(n // g, g, w), axis=0)
    return op(part, axis=0, keepdims=True)


def _head_pair_block(t):
    lane = lax.broadcasted_iota(jnp.int32, t.shape, 1)
    zero = jnp.zeros_like(t)
    return jnp.concatenate([jnp.where(lane < 64, t, zero), jnp.where(lane >= 64, t, zero)], axis=0)


DSA_SLAB = 256
DSA_CNT = 128


def _dsa_body(pc_ref, vt_ref, gt_ref, out_ref, key_ref, hi_ref, lo_ref, bias_ref, l_ref, e_ref,
              *, nk, n_sel, pair_base):
    QT = 2 * CHUNK
    p = pair_base + pl.program_id(1)
    q0 = pl.multiple_of(p * QT, QT)
    idx_scale = (IDX_HEADS ** -0.5) * (IDX_DIM ** -0.5)
    att_scale = ATT_DH ** -0.5

    aq_t = pc_ref[0, pl.ds(q0, QT), 0:256]
    iq_t = pc_ref[0, pl.ds(q0, QT), 512:768]
    iw = gt_ref[0, 16:24, pl.ds(q0, QT)] * idx_scale
    qlane = lax.broadcasted_iota(jnp.int32, (1, QT), 1)
    lim = ((q0 + qlane) // CHUNK + 1) * CHUNK
    slabs = [(r0, min(DSA_SLAB, nk - r0)) for r0 in range(0, nk, DSA_SLAB)]

    iblk = [_head_pair_block(iq_t[:, hp * 128:(hp + 1) * 128]) for hp in range(IDX_HEADS // 2)]
    for r0, n in slabs:
        ik2 = pc_ref[0, r0:r0 + n, 768:896]
        score = jnp.zeros((n, QT), F32)
        for hp in range(IDX_HEADS // 2):
            r = _dot_nt(ik2, iblk[hp])
            score = score + jnp.maximum(r[:, 0:QT], 0.0) * iw[2 * hp:2 * hp + 1, :]
            score = score + jnp.maximum(r[:, QT:2 * QT], 0.0) * iw[2 * hp + 1:2 * hp + 2, :]
        krow = r0 + lax.broadcasted_iota(jnp.int32, (n, QT), 0)
        score = jnp.where(krow < lim, score, NEG_INF)
        bits = pltpu.bitcast(score, jnp.int32)
        key = jnp.where(bits >= 0, bits, jnp.int32(INT_MIN) - bits)
        key_ref[r0:r0 + n, :] = key
        hi_ref[r0:r0 + n, :] = (key >> 16).astype(jnp.int16)
        lo_ref[r0:r0 + n, :] = ((key & 0xFFFF) - 32768).astype(jnp.int16)

    def count_ge(ref, cand):
        c16 = jnp.broadcast_to(cand, (DSA_CNT, QT)).astype(jnp.int16)
        acc = jnp.zeros((DSA_CNT, QT), jnp.int16)
        for r0 in range(0, nk, DSA_CNT):
            acc = acc + jnp.where(ref[r0:r0 + DSA_CNT, :] >= c16, jnp.int16(1), jnp.int16(0))
        return jnp.sum(acc.astype(F32), axis=0, keepdims=True)

    def make_step(ref, target):
        def step(i, carry):
            t, c_t = carry
            cand = t + lax.shift_left(jnp.int32(1), 15 - i)
            cnt = count_ge(ref, cand)
            ok = cnt >= target
            return jnp.where(ok, cand, t), jnp.where(ok, cnt, c_t)
        return step

    t_min = jnp.full((1, QT), -32768, jnp.int32)
    t_hi, _ = lax.fori_loop(0, 16, make_step(hi_ref, float(n_sel)), (t_min, jnp.full((1, QT), float(nk), F32)))
    cnt_hi_gt = count_ge(hi_ref, t_hi + 1)
    hi16 = jnp.broadcast_to(t_hi, (DSA_CNT, QT)).astype(jnp.int16)
    for r0 in range(0, nk, DSA_CNT):
        lo_ref[r0:r0 + DSA_CNT, :] = jnp.where(hi_ref[r0:r0 + DSA_CNT, :] == hi16,
                                               lo_ref[r0:r0 + DSA_CNT, :], jnp.int16(-32768))
    in_bucket = count_ge(hi_ref, t_hi) - cnt_hi_gt
    t_lo, c_lo = lax.fori_loop(0, 16, make_step(lo_ref, n_sel - cnt_hi_gt), (t_min, in_bucket))
    tau = t_hi * 65536 + (t_lo + 32768)
    cnt_ge = cnt_hi_gt + c_lo
    valid = tau > KEY_NEG_INF
    tau_t = jnp.maximum(tau, KEY_NEG_INF)
    drop = jnp.where(valid, cnt_ge - n_sel, float(nk))
    tri_r = lax.broadcasted_iota(jnp.int32, (DSA_SLAB, DSA_SLAB), 0)
    tri_c = lax.broadcasted_iota(jnp.int32, (DSA_SLAB, DSA_SLAB), 1)
    tri = jnp.where(tri_c >= tri_r, 1.0, 0.0).astype(BF16)
    later = jnp.zeros((1, QT), F32)
    for r0, n in reversed(slabs):
        key = key_ref[r0:r0 + n, :]
        eq = jnp.where(key == tau_t, 1.0, 0.0)
        rank_end = _dot(tri[0:n, 0:n], eq.astype(BF16)) + later
        later = later + jnp.sum(eq, axis=0, keepdims=True)
        tie_bias = jnp.where(rank_end > drop, 0.0, NEG_INF)
        bias_ref[r0:r0 + n, :] = jnp.where(key > tau_t, 0.0, jnp.where(key == tau_t, tie_bias, NEG_INF))

    outs = []
    for hp in range(ATT_HEADS // 2):
        q2 = aq_t[:, hp * 128:(hp + 1) * 128] * jnp.asarray(att_scale, BF16)
        blk = _head_pair_block(q2)
        mx = [jnp.full((64, QT), NEG_INF, F32) for _ in range(2)]
        for r0, n in slabs:
            lg = _dot_nt(pc_ref[0, r0:r0 + n, 256 + hp * 128:256 + (hp + 1) * 128], blk)
            b = bias_ref[r0:r0 + n, :]
            for sub in range(2):
                l = lg[:, sub * QT:(sub + 1) * QT] + b
                l_ref[sub, r0:r0 + n, :] = l
                mx[sub] = jnp.maximum(mx[sub], jnp.max(l.reshape(n // 64, 64, QT), axis=0))
        for sub in range(2):
            h = 2 * hp + sub
            m = jnp.max(mx[sub], axis=0, keepdims=True)
            den = jnp.zeros((64, QT), F32)
            for r0, n in slabs:
                e = jnp.exp(l_ref[sub, r0:r0 + n, :] - m)
                den = den + jnp.sum(e.reshape(n // 64, 64, QT), axis=0)
                e_ref[r0:r0 + n, :] = e.astype(BF16)
            o_t = _dot(vt_ref[0, h * ATT_DH:(h + 1) * ATT_DH, 0:nk], e_ref[...])
            outs.append(o_t / jnp.sum(den, axis=0, keepdims=True))
    out_t = jnp.concatenate(outs, axis=0)
    out_ref[0] = out_t.T.astype(BF16)


def _dsa_class(pc, vt, gt, *, nk, n_sel, pair_base, n_pairs):
    B, S, ncols = pc.shape
    QT = 2 * CHUNK
    return pl.pallas_call(
        functools.partial(_dsa_body, nk=nk, n_sel=n_sel, pair_base=pair_base),
        grid=(B, n_pairs),
        in_specs=[
            pl.BlockSpec((1, nk, ncols), lambda b, i: (b, 0, 0)),
            pl.BlockSpec((1, 256, nk), lambda b, i: (b, 0, 0)),
            pl.BlockSpec((1, 24, nk), lambda b, i: (b, 0, 0)),
        ],
        out_specs=pl.BlockSpec((1, QT, 256), lambda b, i: (b, i, 0)),
        out_shape=jax.ShapeDtypeStruct((B, n_pairs * QT, 256), BF16),
        scratch_shapes=[
            pltpu.VMEM((nk, QT), jnp.int32),
            pltpu.VMEM((nk, QT), jnp.int16),
            pltpu.VMEM((nk, QT), jnp.int16),
            pltpu.VMEM((nk, QT), F32),
            pltpu.VMEM((2, nk, QT), F32),
            pltpu.VMEM((nk, QT), BF16),
        ],
        compiler_params=_cparams(("parallel", "parallel")),
        name=f"dsa_{nk}",
    )(pc, vt, gt)


def _dsa(pc, vt, gt, n_classes=8):
    B, S, _ = pc.shape
    QT = 2 * CHUNK
    n_pairs = S // QT
    n_sel = min(TOPK_MAX, S // 4)
    n_classes = min(n_classes, n_pairs)
    per = n_pairs // n_classes
    outs = []
    for c in range(n_classes):
        nk = (c + 1) * per * QT
        outs.append(_dsa_class(pc, vt, gt, nk=nk, n_sel=n_sel, pair_base=c * per, n_pairs=per))
    return jnp.concatenate(outs, axis=1)


def _outproj_body(x_ref, ha_ref, hb_ref, hc_ref, wa_ref, wb_ref, wc_ref, g_ref, o_ref):
    mix = _dot(ha_ref[0], wa_ref[...]) + _dot(hb_ref[0], wb_ref[...]) + _dot(hc_ref[0], wc_ref[...])
    ms = jnp.mean(mix * mix, axis=-1, keepdims=True)
    o_ref[0] = x_ref[0] + mix * lax.rsqrt(ms + EPS) * g_ref[...]


def _outproj(x, ha, hb, hc, wa, wb, wc, gain, *, tm):
    B, S, D = x.shape
    row = lambda n: pl.BlockSpec((1, tm, n), lambda b, i: (b, i, 0))
    full = lambda a: pl.BlockSpec(a.shape, lambda b, i: (0, 0))
    return pl.pallas_call(
        _outproj_body,
        grid=(B, S // tm),
        in_specs=[row(D), row(ha.shape[2]), row(hb.shape[2]), row(hc.shape[2]),
                  full(wa), full(wb), full(wc), full(gain)],
        out_specs=row(D),
        out_shape=jax.ShapeDtypeStruct((B, S, D), F32),
        compiler_params=_cparams(("parallel", "parallel")),
        name="out_proj",
    )(x, ha, hb, hc, wa, wb, wc, gain)


def _ffn_body(x_ref, gpre_ref, wup_ref, cw_ref, cb_ref, wd_ref, gpost_ref, o_ref, h_ref, act_ref, halo_ref,
              *, tm, dff, tn, rb):
    i = pl.program_id(1)
    x = x_ref[0]
    ms = jnp.mean(x * x, axis=-1, keepdims=True)
    h_ref[...] = (x * lax.rsqrt(ms + EPS) * gpre_ref[...]).astype(BF16)

    @pl.when(i == 0)
    def _():
        halo_ref[...] = jnp.zeros_like(halo_ref)

    def row_block(r, carry):
        r0 = pl.multiple_of(r * rb, rb)
        hrows = h_ref[pl.ds(r0, rb), :]

        def conv(col0, slot, c0):
            xu = _dot(hrows, wup_ref[:, col0:col0 + tn])
            prev = halo_ref[slot, :, c0:c0 + tn]
            halo_ref[slot, :, c0:c0 + tn] = xu[rb - 8:rb]
            cat = jnp.concatenate([prev, xu], axis=0)
            y = xu * cw_ref[FFN_CONV - 1:FFN_CONV, col0:col0 + tn] + cb_ref[:, col0:col0 + tn]
            for d in range(1, FFN_CONV):
                y = y + pltpu.roll(cat, d, axis=0)[8:8 + rb] * cw_ref[FFN_CONV - 1 - d:FFN_CONV - d, col0:col0 + tn]
            return y

        for c0 in range(0, dff, tn):
            gate = conv(c0, 0, c0)
            up = conv(dff + c0, 1, c0)
            act_ref[:, c0:c0 + tn] = (_gelu_tanh(gate) * up).astype(BF16)
        y = _dot(act_ref[...], wd_ref[...])
        ms_y = jnp.mean(y * y, axis=-1, keepdims=True)
        o_ref[0, pl.ds(r0, rb), :] = x_ref[0, pl.ds(r0, rb), :] + y * lax.rsqrt(ms_y + EPS) * gpost_ref[...]
        return carry

    lax.fori_loop(0, tm // rb, row_block, 0)


def _ffn(x, gpre, w_up, cw, cb, w_down, gpost, *, tm, tn, rb):
    B, S, D = x.shape
    dff = w_down.shape[0]
    resident = lambda a: pl.BlockSpec(a.shape, lambda b, i: (0, 0), pipeline_mode=pl.Buffered(1))
    return pl.pallas_call(
        functools.partial(_ffn_body, tm=tm, dff=dff, tn=tn, rb=rb),
        grid=(B, S // tm),
        in_specs=[
            pl.BlockSpec((1, tm, D), lambda b, i: (b, i, 0)),
            resident(gpre), resident(w_up), resident(cw), resident(cb), resident(w_down), resident(gpost),
        ],
        out_specs=pl.BlockSpec((1, tm, D), lambda b, i: (b, i, 0)),
        out_shape=jax.ShapeDtypeStruct((B, S, D), F32),
        scratch_shapes=[
            pltpu.VMEM((tm, D), BF16),
            pltpu.VMEM((rb, dff), BF16),
            pltpu.VMEM((2, 8, dff), F32),
        ],
        compiler_params=_cparams(("parallel", "arbitrary")),
        name="conv_ffn",
    )(x, gpre, w_up, cw, cb, w_down, gpost)


def _pad_heads(w, n_heads, dh, pad):
    lead = w.shape[:-1]
    w = w.reshape(lead + (n_heads, dh))
    w = jnp.pad(w, [(0, 0)] * len(lead) + [(0, 0), (0, pad - dh)])
    return w.reshape(lead + (n_heads * pad,))


def _block_diag(w):
    nb, bw, _ = w.shape
    eye = jnp.eye(nb, dtype=w.dtype)
    return jnp.einsum("ncd,nm->ncmd", w, eye).reshape(nb * bw, nb * bw)


def _project(x, p):
    B, S, D = x.shape
    H, DH, P = MLSTM_HEADS, MLSTM_DH, HEAD_PAD
    mw = H * DH
    lw = p["lru_lambda"].shape[-1]
    aw = ATT_HEADS * ATT_DH
    w_in = p["w_in"]
    offs = [0]
    for s in (mw, mw, mw, mw, H, H, lw, lw, aw, aw, aw, IDX_HEADS * IDX_DIM, IDX_DIM, IDX_HEADS):
        offs.append(offs[-1] + s)
    col = lambda k: w_in[:, offs[k]:offs[k + 1]]
    mq, mk, mv, mo, mi, mf, lx, lg, aq, ak, av, iq, ik, iw = [col(k) for k in range(14)]

    w_a = jnp.concatenate([_pad_heads(t, H, DH, P) for t in (mq, mk, mv, mo)], axis=1)
    w_l = jnp.concatenate([lx, lg], axis=1)
    w_c = jnp.concatenate([aq, ak, iq, ik, ik], axis=1)
    w_main = jnp.concatenate([w_a, w_l, w_c], axis=1).astype(BF16)
    na, nl, nc = w_a.shape[1], w_l.shape[1], w_c.shape[1]
    pad4 = lambda t: jnp.pad(t, ((0, 0), (0, 4)))
    w_t = jnp.concatenate([av, pad4(mi), pad4(mf), pad4(iw)], axis=1).T.astype(BF16)

    return _proj(x, p["norm_mix_pre"].reshape(1, D), w_main, w_t, na=na, nl=nl, nc=nc, tm=min(512, S))


def _layer_parts(x, p):
    H, DH, P = MLSTM_HEADS, MLSTM_DH, HEAD_PAD
    mw = H * DH
    lw = p["lru_lambda"].shape[-1]
    pa, plx, pc, vt, gt = _project(x, p)

    pad_col = lambda b: jnp.pad(b.reshape(H, 1), ((0, 8 - H), (0, 0)))
    ha, hb = _mixers(pa, gt, pad_col(p["b_igate"]), pad_col(p["b_fgate"]),
                     _pad_heads(p["mlstm_norm"].reshape(1, mw), H, DH, P),
                     plx, p["lru_conv_w"], p["lru_conv_b"].reshape(1, lw),
                     _block_diag(p["lru_w_a"]).astype(BF16), p["lru_b_a"].reshape(1, lw),
                     _block_diag(p["lru_w_x"]).astype(BF16), p["lru_b_x"].reshape(1, lw),
                     p["lru_lambda"].reshape(1, lw))

    hc = _dsa(pc, vt, gt)
    return ha, hb, hc


def _layer(x, p):
    B, S, D = x.shape
    H, DH, P = MLSTM_HEADS, MLSTM_DH, HEAD_PAD
    mw = H * DH
    lw = p["lru_lambda"].shape[-1]
    ha, hb, hc = _layer_parts(x, p)

    w_out = p["w_out"]
    wo_a = _pad_heads(w_out[0:mw].T, H, DH, P).T.astype(BF16)
    wo_b = w_out[mw:mw + lw].astype(BF16)
    wo_c = w_out[mw + lw:].astype(BF16)
    x = _outproj(x, ha, hb, hc, wo_a, wo_b, wo_c, p["norm_mix_post"].reshape(1, D), tm=min(512, S))

    dff2 = p["ffn_up"].shape[1]
    x = _ffn(x, p["norm_ffn_pre"].reshape(1, D), p["ffn_up"].astype(BF16), p["ffn_conv_w"],
             p["ffn_conv_b"].reshape(1, dff2), p["ffn_down"].astype(BF16), p["norm_ffn_post"].reshape(1, D),
             tm=min(1024, S), tn=256, rb=256)
    return x


def kernel(x, norm_mix_pre, norm_mix_post, norm_ffn_pre, norm_ffn_post, w_in, b_igate, b_fgate,
           mlstm_norm, lru_conv_w, lru_conv_b, lru_w_a, lru_b_a, lru_w_x, lru_b_x, lru_lambda,
           w_out, ffn_up, ffn_conv_w, ffn_conv_b, ffn_down):
    params = dict(norm_mix_pre=norm_mix_pre, norm_mix_post=norm_mix_post, norm_ffn_pre=norm_ffn_pre,
                  norm_ffn_post=norm_ffn_post, w_in=w_in, b_igate=b_igate, b_fgate=b_fgate,
                  mlstm_norm=mlstm_norm, lru_conv_w=lru_conv_w, lru_conv_b=lru_conv_b,
                  lru_w_a=lru_w_a, lru_b_a=lru_b_a, lru_w_x=lru_w_x, lru_b_x=lru_b_x,
                  lru_lambda=lru_lambda, w_out=w_out, ffn_up=ffn_up, ffn_conv_w=ffn_conv_w,
                  ffn_conv_b=ffn_conv_b, ffn_down=ffn_down)
    for l in range(w_in.shape[0]):
        x = _layer(x, {k: v[l] for k, v in params.items()})
    return x
```

```python
import functools

import jax
import jax.numpy as jnp
from jax import lax
from jax.experimental import pallas as pl
from jax.experimental.pallas import tpu as pltpu

F32 = jnp.float32
BF16 = jnp.bfloat16

LANES = 128
CHUNK = 64
MLSTM_HEADS = 4
MLSTM_DH = 96
HEAD_PAD = LANES
LRU_BLOCKS = 6
LRU_CONV = 4
LRU_TILE = 64
RG_C = 8.0
ATT_HEADS = 4
ATT_DH = 64
IDX_HEADS = 4
IDX_DIM = 64
TOPK_MAX = 256
FFN_CONV = 3
EPS = 1e-6
VMEM_LIMIT = 56 * 1024 * 1024

NEG_INF = float("-inf")
INT_MIN = -2147483648
KEY_NEG_INF = -2139095040


def _cparams(sem):
    return pltpu.CompilerParams(dimension_semantics=sem, vmem_limit_bytes=VMEM_LIMIT)


def _dot(a, b):
    return jnp.dot(a, b, preferred_element_type=F32)


def _dot_nt(a, b):
    return lax.dot_general(a, b, (((1,), (1,)), ((), ())), preferred_element_type=F32)


def _split_dot(x, ones_b):
    hi = x.astype(BF16)
    r1 = x - hi.astype(F32)
    mid = r1.astype(BF16)
    lo = (r1 - mid.astype(F32)).astype(BF16)
    return _dot(hi, ones_b) + _dot(mid, ones_b) + _dot(lo, ones_b)


def _log_sigmoid(x):
    return jnp.minimum(x, 0.0) - jnp.log1p(jnp.exp(-jnp.abs(x)))


def _sigmoid(x):
    return 0.5 * jnp.tanh(0.5 * x) + 0.5


def _one_minus_sq_exp(log_a, a):
    x = 2.0 * log_a
    series = -x * (1.0 + x * (1 / 2 + x * (1 / 6 + x * (1 / 24 + x * (1 / 120)))))
    return jnp.where(x > -0.1, series, 1.0 - a * a)


def _gelu_tanh(x):
    c = 0.7978845608028654
    return 0.5 * x * (1.0 + jnp.tanh(c * (x + 0.044715 * (x * x * x))))


def _proj_body(x_ref, g_ref, w_ref, wt_ref, pa_ref, pl_ref, pc_ref, vt_ref, gt_ref,
               *, na, nl, nc):
    x = x_ref[0]
    ms = jnp.mean(x * x, axis=-1, keepdims=True)
    h = (x * lax.rsqrt(ms + EPS) * g_ref[...]).astype(BF16)
    step = 512
    for ref, base, width in ((pa_ref, 0, na), (pl_ref, na, nl), (pc_ref, na + nl, nc)):
        for lo in range(0, width, step):
            hi = min(lo + step, width)
            ref[0, :, lo:hi] = _dot(h, w_ref[:, base + lo:base + hi]).astype(ref.dtype)
    t = _dot_nt(wt_ref[...], h)
    vt_ref[0] = t[0:256].astype(BF16)
    gt_ref[0] = t[256:280]


def _proj(x, gain, w, wt, *, na, nl, nc, tm):
    B, S, D = x.shape
    nw = w.shape[1]
    grid = (B, S // tm)
    return pl.pallas_call(
        functools.partial(_proj_body, na=na, nl=nl, nc=nc),
        grid=grid,
        in_specs=[
            pl.BlockSpec((1, tm, D), lambda b, i: (b, i, 0)),
            pl.BlockSpec((1, D), lambda b, i: (0, 0)),
            pl.BlockSpec((D, nw), lambda b, i: (0, 0)),
            pl.BlockSpec((wt.shape[0], D), lambda b, i: (0, 0)),
        ],
        out_specs=[
            pl.BlockSpec((1, tm, na), lambda b, i: (b, i, 0)),
            pl.BlockSpec((1, tm, nl), lambda b, i: (b, i, 0)),
            pl.BlockSpec((1, tm, nc), lambda b, i: (b, i, 0)),
            pl.BlockSpec((1, 256, tm), lambda b, i: (b, 0, i)),
            pl.BlockSpec((1, 24, tm), lambda b, i: (b, 0, i)),
        ],
        out_shape=[
            jax.ShapeDtypeStruct((B, S, na), BF16),
            jax.ShapeDtypeStruct((B, S, nl), F32),
            jax.ShapeDtypeStruct((B, S, nc), BF16),
            jax.ShapeDtypeStruct((B, 256, S), BF16),
            jax.ShapeDtypeStruct((B, 24, S), F32),
        ],
        compiler_params=_cparams(("parallel", "parallel")),
        name="in_proj",
    )(x, gain, w, wt)


def _mixers_body(pa_ref, gt_ref, bi_ref, bf_ref, gain_ref,
                 pl_ref, cw_ref, cb_ref, wa_ref, ba_ref, wx_ref, bx_ref, lam_ref,
                 out_ref, outb_ref, c_ref, row_ref, bc_ref, rm_ref, mm_ref, *, n_pairs):
    H, P, L = MLSTM_HEADS, HEAD_PAD, CHUNK
    scale = MLSTM_DH ** -0.5
    c_ref[...] = jnp.zeros_like(c_ref)
    lru_refs = (pl_ref, cw_ref, cb_ref, wa_ref, ba_ref, wx_ref, bx_ref)
    sp = _lru_softplus_neg(lam_ref[...])

    r128 = lax.broadcasted_iota(jnp.int32, (2 * L, 2 * L), 0)
    c128 = lax.broadcasted_iota(jnp.int32, (2 * L, 2 * L), 1)
    triu_bd = jnp.where((r128 // L == c128 // L) & (r128 <= c128), 1.0, 0.0).astype(BF16)
    r64 = lax.broadcasted_iota(jnp.int32, (L, L), 0)
    c64 = lax.broadcasted_iota(jnp.int32, (L, L), 1)
    tril = c64 <= r64
    lane_p = lax.broadcasted_iota(jnp.int32, (L, P), 1)
    is_den = lane_p == MLSTM_DH
    is_feat = lane_p < MLSTM_DH
    is_last = lax.broadcasted_iota(jnp.int32, (8, L), 1) == L - 1

    def gate_step(j, carry):
        m_col, h_lru = carry
        h_lru = _lru_tile(lru_refs, outb_ref, sp, j, h_lru)
        col0 = pl.multiple_of(j * (2 * L), 2 * L)
        gi = gt_ref[0, 0:8, pl.ds(col0, 2 * L)] + bi_ref[...]
        lf = _log_sigmoid(gt_ref[0, 8:16, pl.ds(col0, 2 * L)] + bf_ref[...])
        bcum = _split_dot(lf, triu_bd)
        row_ref[0, :, pl.ds(col0, 2 * L)] = gi
        row_ref[1, :, pl.ds(col0, 2 * L)] = bcum
        for half in range(2):
            c = 2 * j + half
            row0 = pl.multiple_of(c * L, L)
            lsl = slice(half * L, (half + 1) * L)
            gi_h, bcum_h, lf_h = gi[:, lsl], bcum[:, lsl], lf[:, lsl]
            mm_ref[c] = jnp.broadcast_to(m_col, (8, P))
            b_last = jnp.sum(jnp.where(is_last, bcum_h, 0.0), axis=1, keepdims=True)
            gmax = jnp.max(b_last - bcum_h + gi_h, axis=1, keepdims=True)
            m_col = jnp.maximum(b_last + m_col, gmax)
            for h in range(H):
                lrow = jnp.broadcast_to(lf_h[h:h + 1], (L, L))
                bc = jnp.sum(jnp.where(tril, lrow, 0.0), axis=1, keepdims=True)
                d_log = jnp.where(tril, bc - bcum_h[h:h + 1] + gi_h[h:h + 1], NEG_INF)
                bc_ref[h, pl.ds(row0, L), :] = jnp.broadcast_to(bc, (L, P))
                rm_ref[h, pl.ds(row0, L), :] = jnp.broadcast_to(jnp.max(d_log, axis=1, keepdims=True), (L, P))
        return m_col, h_lru

    lru_c = cw_ref.shape[1]
    m_end, h_lru = lax.fori_loop(0, n_pairs, gate_step,
                                 (jnp.zeros((8, 1), F32), jnp.zeros((1, lru_c), F32)))
    mm_ref[2 * n_pairs] = jnp.broadcast_to(m_end, (8, P))

    def pair_step(j, h_lru):
        h_lru = _lru_tile(lru_refs, outb_ref, sp, n_pairs + j, h_lru)
        col0 = pl.multiple_of(j * (2 * L), 2 * L)
        gi = row_ref[0, :, pl.ds(col0, 2 * L)]
        bcum = row_ref[1, :, pl.ds(col0, 2 * L)]
        for half in range(2):
            c = 2 * j + half
            row0 = pl.multiple_of(c * L, L)
            lsl = slice(half * L, (half + 1) * L)
            gi_h, bcum_h = gi[:, lsl], bcum[:, lsl]
            m_prev_all = mm_ref[c]
            m_new_all = mm_ref[c + 1]
            for h in range(H):
                hs = slice(h * P, (h + 1) * P)
                irow = gi_h[h:h + 1]
                brow = bcum_h[h:h + 1]
                m_prev = m_prev_all[h:h + 1]
                m_new = m_new_all[h:h + 1]
                bc = bc_ref[h, pl.ds(row0, L), :]
                d_log = jnp.where(tril, bc[:, 0:L] - brow + irow, NEG_INF)
                inter = bc + m_prev
                m_t = jnp.maximum(inter, rm_ref[h, pl.ds(row0, L), :])
                w_intra = jnp.exp(d_log - m_t[:, 0:L])
                w_inter = jnp.exp(inter - m_t)

                q = pa_ref[0, pl.ds(row0, L), h * P:(h + 1) * P]
                k = pa_ref[0, pl.ds(row0, L), (H + h) * P:(H + h + 1) * P]
                v = pa_ref[0, pl.ds(row0, L), (2 * H + h) * P:(2 * H + h + 1) * P]
                o = pa_ref[0, pl.ds(row0, L), (3 * H + h) * P:(3 * H + h + 1) * P]
                v_aug = jnp.where(is_den, 1.0, v.astype(F32)).astype(BF16)
                kT = k.astype(F32).T

                s = _dot(q, kT.astype(BF16)) * (w_intra * scale)
                c_prev = c_ref[h]
                num = w_inter * _dot(q, c_prev.astype(BF16)) + _dot(s.astype(BF16), v_aug)
                den = jnp.sum(jnp.where(is_den, num, 0.0), axis=1, keepdims=True)
                feat = jnp.where(is_feat, num, 0.0)
                ssq = jnp.sum(feat * feat, axis=1, keepdims=True)
                dd = jnp.maximum(jnp.abs(den), jnp.exp(-m_t))
                inv = 1.0 / dd
                rs = inv * lax.rsqrt(ssq * (inv * inv) * (1.0 / MLSTM_DH) + EPS)
                y = feat * rs * gain_ref[:, hs] * _sigmoid(o.astype(F32))
                out_ref[0, pl.ds(row0, L), hs] = y.astype(BF16)

                b_last = bc[L - 1:L, :]
                grow = b_last[:, 0:L] - brow + irow
                wg = jnp.exp(grow - m_new[:, 0:L]) * scale
                decay = jnp.exp(b_last + m_prev - m_new)
                c_ref[h] = decay * c_prev + _dot((kT * wg).astype(BF16), v_aug)
        return h_lru

    lax.fori_loop(0, n_pairs, pair_step, h_lru)


def _mixers(pa, gt, bi, bf, gain, plx, cw, cb, wa, ba, wx, bx, lam):
    B, S, _ = pa.shape
    H, P = MLSTM_HEADS, HEAD_PAD
    C = lam.shape[1]
    n_pairs = S // (2 * CHUNK)
    assert S // LRU_TILE == 2 * n_pairs
    full = lambda a: pl.BlockSpec(a.shape, lambda b: (0,) * a.ndim)
    return pl.pallas_call(
        functools.partial(_mixers_body, n_pairs=n_pairs),
        grid=(B,),
        in_specs=[
            pl.BlockSpec((1, S, 4 * H * P), lambda b: (b, 0, 0)),
            pl.BlockSpec((1, 24, S), lambda b: (b, 0, 0)),
            full(bi), full(bf), full(gain),
            pl.BlockSpec((1, S, 2 * C), lambda b: (b, 0, 0)),
            full(cw), full(cb), full(wa), full(ba), full(wx), full(bx), full(lam),
        ],
        out_specs=[pl.BlockSpec((1, S, H * P), lambda b: (b, 0, 0)),
                   pl.BlockSpec((1, S, C), lambda b: (b, 0, 0))],
        out_shape=[jax.ShapeDtypeStruct((B, S, H * P), BF16),
                   jax.ShapeDtypeStruct((B, S, C), BF16)],
        scratch_shapes=[
            pltpu.VMEM((H, P, P), F32),
            pltpu.VMEM((2, 8, S), F32),
            pltpu.VMEM((H, S, P), F32),
            pltpu.VMEM((H, S, P), F32),
            pltpu.VMEM((2 * n_pairs + 1, 8, P), F32),
        ],
        compiler_params=_cparams(("parallel",)),
        name="mixers_ab",
    )(pa, gt, bi, bf, gain, plx, cw, cb, wa, ba, wx, bx, lam)


def _lru_softplus_neg(lam):
    neg_lam = -lam
    return jnp.maximum(neg_lam, 0.0) + jnp.log1p(jnp.exp(-jnp.abs(neg_lam)))


def _lru_tile(lru_refs, out_ref, sp, t, h_last):
    pl_ref, cw_ref, cb_ref, wa_ref, ba_ref, wx_ref, bx_ref = lru_refs
    tr = LRU_TILE
    C = cw_ref.shape[1]
    sub = lax.broadcasted_iota(jnp.int32, (tr, C), 0) & 7
    r0 = pl.multiple_of(t * tr, tr)
    x = pl_ref[0, pl.ds(r0, tr), 0:C]
    lg = pl_ref[0, pl.ds(r0, tr), C:2 * C]
    rp = pl.multiple_of(jnp.maximum(r0 - 8, 0), 8)
    xp = pl_ref[0, pl.ds(rp, 8), 0:C] * jnp.where(t > 0, 1.0, 0.0)
    cat = jnp.concatenate([xp, x], axis=0)
    xc = x * cw_ref[LRU_CONV - 1:LRU_CONV, :] + cb_ref[...]
    for d in range(1, LRU_CONV):
        xs = pltpu.roll(cat, d, axis=0)[8:8 + tr]
        xc = xc + xs * cw_ref[LRU_CONV - 1 - d:LRU_CONV - d, :]
    xcb = xc.astype(BF16)
    r = _sigmoid(_dot(xcb, wa_ref[...]) + ba_ref[...])
    i = _sigmoid(_dot(xcb, wx_ref[...]) + bx_ref[...])
    log_a = -RG_C * r * sp
    a = jnp.exp(log_a)
    u = jnp.sqrt(_one_minus_sq_exp(log_a, a)) * (i * xc)
    for d in (1, 2, 4):
        a_s = pltpu.roll(a, d, axis=0)
        u_s = pltpu.roll(u, d, axis=0)
        ok = sub >= d
        u = jnp.where(ok, u + a * u_s, u)
        a = jnp.where(ok, a * a_s, a)
    outs = []
    carry = h_last
    for g in range(tr // 8):
        hg = u[g * 8:(g + 1) * 8] + a[g * 8:(g + 1) * 8] * carry
        carry = hg[7:8]
        outs.append(hg)
    hfull = jnp.concatenate(outs, axis=0)
    out_ref[0, pl.ds(r0, tr), :] = (hfull * _gelu_tanh(lg)).astype(BF16)
    return carry


def _col_reduce(x, op):
    n, w = x.shape
    g = 64 if n % 64 == 0 else 8
    part = op(x.reshape(n // g, g, w), axis=0)
    return op(part, axis=0, keepdims=True)


def _head_pair_block(t):
    lane = lax.broadcasted_iota(jnp.int32, t.shape, 1)
    zero = jnp.zeros_like(t)
    return jnp.concatenate([jnp.where(lane < 64, t, zero), jnp.where(lane >= 64, t, zero)], axis=0)


DSA_SLAB = 256
DSA_QT = 256


def _dsa_body(pc_ref, vt_ref, gt_ref, out_ref, key_ref, bias_ref, l_ref, e_ref, *, nk, n_sel, q0):
    QT = DSA_QT
    idx_scale = (IDX_HEADS ** -0.5) * (IDX_DIM ** -0.5)
    att_scale = ATT_DH ** -0.5

    aq_t = pc_ref[0, q0:q0 + QT, 0:256]
    iq_t = pc_ref[0, q0:q0 + QT, 512:768]
    iw = gt_ref[0, 16:24, q0:q0 + QT] * idx_scale
    qlane = lax.broadcasted_iota(jnp.int32, (1, QT), 1)
    lim = ((q0 + qlane) // CHUNK + 1) * CHUNK
    slabs = [(r0, min(DSA_SLAB, nk - r0)) for r0 in range(0, nk, DSA_SLAB)]

    iblk = [_head_pair_block(iq_t[:, hp * 128:(hp + 1) * 128]) for hp in range(IDX_HEADS // 2)]
    for r0, n in slabs:
        ik2 = pc_ref[0, r0:r0 + n, 768:896]
        score = jnp.zeros((n, QT), F32)
        for hp in range(IDX_HEADS // 2):
            r = _dot_nt(ik2, iblk[hp])
            score = score + jnp.maximum(r[:, 0:QT], 0.0) * iw[2 * hp:2 * hp + 1, :]
            score = score + jnp.maximum(r[:, QT:2 * QT], 0.0) * iw[2 * hp + 1:2 * hp + 2, :]
        krow = r0 + lax.broadcasted_iota(jnp.int32, (n, QT), 0)
        score = jnp.where(krow < lim, score, NEG_INF)
        bits = pltpu.bitcast(score, jnp.int32)
        key_ref[r0:r0 + n, :] = jnp.where(bits >= 0, bits, jnp.int32(INT_MIN) - bits)

    def bit_step(i, carry):
        t, c_t = carry
        cand = t + lax.shift_left(jnp.int32(1), 31 - i)
        acc = jnp.zeros((64, QT), F32)
        for r0 in range(0, nk, 64):
            acc = acc + jnp.where(key_ref[r0:r0 + 64, :] >= cand, 1.0, 0.0)
        cnt = jnp.sum(acc, axis=0, keepdims=True)
        ok = cnt >= n_sel
        return jnp.where(ok, cand, t), jnp.where(ok, cnt, c_t)

    tau, cnt_ge = lax.fori_loop(0, 32, bit_step,
                                (jnp.full((1, QT), INT_MIN, jnp.int32), jnp.full((1, QT), float(nk), F32)))
    valid = tau > KEY_NEG_INF
    tau_t = jnp.maximum(tau, KEY_NEG_INF)
    drop = jnp.where(valid, cnt_ge - n_sel, float(nk))
    tri_r = lax.broadcasted_iota(jnp.int32, (DSA_SLAB, DSA_SLAB), 0)
    tri_c = lax.broadcasted_iota(jnp.int32, (DSA_SLAB, DSA_SLAB), 1)
    tri = jnp.where(tri_c >= tri_r, 1.0, 0.0).astype(BF16)
    later = jnp.zeros((1, QT), F32)
    for r0, n in reversed(slabs):
        key = key_ref[r0:r0 + n, :]
        eq = jnp.where(key == tau_t, 1.0, 0.0)
        rank_end = _dot(tri[0:n, 0:n], eq.astype(BF16)) + later
        later = later + jnp.sum(eq, axis=0, keepdims=True)
        tie_bias = jnp.where(rank_end > drop, 0.0, NEG_INF)
        bias_ref[r0:r0 + n, :] = jnp.where(key > tau_t, 0.0, jnp.where(key == tau_t, tie_bias, NEG_INF))

    outs = []
    for hp in range(ATT_HEADS // 2):
        q2 = aq_t[:, hp * 128:(hp + 1) * 128] * jnp.asarray(att_scale, BF16)
        blk = _head_pair_block(q2)
        mx = [jnp.full((64, QT), NEG_INF, F32) for _ in range(2)]
        for r0, n in slabs:
            lg = _dot_nt(pc_ref[0, r0:r0 + n, 256 + hp * 128:256 + (hp + 1) * 128], blk)
            b = bias_ref[r0:r0 + n, :]
            for sub in range(2):
                l = lg[:, sub * QT:(sub + 1) * QT] + b
                l_ref[sub, r0:r0 + n, :] = l
                mx[sub] = jnp.maximum(mx[sub], jnp.max(l.reshape(n // 64, 64, QT), axis=0))
        for sub in range(2):
            h = 2 * hp + sub
            m = jnp.max(mx[sub], axis=0, keepdims=True)
            den = jnp.zeros((64, QT), F32)
            for r0, n in slabs:
                e = jnp.exp(l_ref[sub, r0:r0 + n, :] - m)
                den = den + jnp.sum(e.reshape(n // 64, 64, QT), axis=0)
                e_ref[r0:r0 + n, :] = e.astype(BF16)
            o_t = _dot(vt_ref[0, h * ATT_DH:(h + 1) * ATT_DH, 0:nk], e_ref[...])
            outs.append(o_t / jnp.sum(den, axis=0, keepdims=True))
    out_t = jnp.concatenate(outs, axis=0)
    out_ref[0] = out_t.T.astype(BF16)


def _dsa_tile(pc, vt, gt, *, nk, n_sel):
    B, S, ncols = pc.shape
    QT = DSA_QT
    return pl.pallas_call(
        functools.partial(_dsa_body, nk=nk, n_sel=n_sel, q0=nk - QT),
        grid=(B,),
        in_specs=[
            pl.BlockSpec((1, nk, ncols), lambda b: (b, 0, 0)),
            pl.BlockSpec((1, 256, nk), lambda b: (b, 0, 0)),
            pl.BlockSpec((1, 24, nk), lambda b: (b, 0, 0)),
        ],
        out_specs=pl.BlockSpec((1, QT, 256), lambda b: (b, 0, 0)),
        out_shape=jax.ShapeDtypeStruct((B, QT, 256), BF16),
        scratch_shapes=[
            pltpu.VMEM((nk, QT), jnp.int32),
            pltpu.VMEM((nk, QT), F32),
            pltpu.VMEM((2, nk, QT), F32),
            pltpu.VMEM((nk, QT), BF16),
        ],
        compiler_params=_cparams(("parallel",)),
        name=f"dsa_{nk}",
    )(pc, vt, gt)


def _dsa(pc, vt, gt):
    B, S, _ = pc.shape
    n_sel = min(TOPK_MAX, S // 4)
    outs = [_dsa_tile(pc, vt, gt, nk=nk, n_sel=n_sel) for nk in range(DSA_QT, S + 1, DSA_QT)]
    return jnp.concatenate(outs, axis=1)


def _ffn_body(x_ref, ha_ref, hb_ref, hc_ref, woa_ref, wob_ref, woc_ref, gmix_ref,
              gpre_ref, wup_ref, cw_ref, cb_ref, wd_ref, gpost_ref, o_ref, h_ref, act_ref, halo_ref,
              *, tm, dff, tn, rb):
    i = pl.program_id(1)

    @pl.when(i == 0)
    def _():
        halo_ref[...] = jnp.zeros_like(halo_ref)

    def row_block(r, carry):
        r0 = pl.multiple_of(r * rb, rb)
        rows = pl.ds(r0, rb)
        mix = (_dot(ha_ref[0, rows, :], woa_ref[...]) + _dot(hb_ref[0, rows, :], wob_ref[...])
               + _dot(hc_ref[0, rows, :], woc_ref[...]))
        ms_mix = jnp.mean(mix * mix, axis=-1, keepdims=True)
        x1 = x_ref[0, rows, :] + mix * lax.rsqrt(ms_mix + EPS) * gmix_ref[...]
        o_ref[0, rows, :] = x1
        ms = jnp.mean(x1 * x1, axis=-1, keepdims=True)
        h_ref[...] = (x1 * lax.rsqrt(ms + EPS) * gpre_ref[...]).astype(BF16)

        def conv(col0, slot, c0):
            xu = _dot(h_ref[...], wup_ref[:, col0:col0 + tn])
            prev = halo_ref[slot, :, c0:c0 + tn]
            halo_ref[slot, :, c0:c0 + tn] = xu[rb - 8:rb]
            cat = jnp.concatenate([prev, xu], axis=0)
            y = xu * cw_ref[FFN_CONV - 1:FFN_CONV, col0:col0 + tn] + cb_ref[:, col0:col0 + tn]
            for d in range(1, FFN_CONV):
                y = y + pltpu.roll(cat, d, axis=0)[8:8 + rb] * cw_ref[FFN_CONV - 1 - d:FFN_CONV - d, col0:col0 + tn]
            return y

        for c0 in range(0, dff, tn):
            gate = conv(c0, 0, c0)
            up = conv(dff + c0, 1, c0)
            act_ref[:, c0:c0 + tn] = (_gelu_tanh(gate) * up).astype(BF16)
        y = _dot(act_ref[...], wd_ref[...])
        ms_y = jnp.mean(y * y, axis=-1, keepdims=True)
        o_ref[0, rows, :] = o_ref[0, rows, :] + y * lax.rsqrt(ms_y + EPS) * gpost_ref[...]
        return carry

    lax.fori_loop(0, tm // rb, row_block, 0)


def _mix_ffn(x, ha, hb, hc, wo_a, wo_b, wo_c, gmix, gpre, w_up, cw, cb, w_down, gpost, *, tm, tn, rb):
    B, S, D = x.shape
    dff = w_down.shape[0]
    row = lambda a: pl.BlockSpec((1, tm, a.shape[2]), lambda b, i: (b, i, 0))
    resident = lambda a: pl.BlockSpec(a.shape, lambda b, i: (0, 0), pipeline_mode=pl.Buffered(1))
    weights = (wo_a, wo_b, wo_c, gmix, gpre, w_up, cw, cb, w_down, gpost)
    return pl.pallas_call(
        functools.partial(_ffn_body, tm=tm, dff=dff, tn=tn, rb=rb),
        grid=(B, S // tm),
        in_specs=[row(x), row(ha), row(hb), row(hc)] + [resident(w) for w in weights],
        out_specs=row(x),
        out_shape=jax.ShapeDtypeStruct((B, S, D), F32),
        scratch_shapes=[
            pltpu.VMEM((rb, D), BF16),
            pltpu.VMEM((rb, dff), BF16),
            pltpu.VMEM((2, 8, dff), F32),
        ],
        compiler_params=_cparams(("parallel", "arbitrary")),
        name="mix_ffn",
    )(x, ha, hb, hc, *weights)


def _pad_heads(w, n_heads, dh, pad):
    lead = w.shape[:-1]
    w = w.reshape(lead + (n_heads, dh))
    w = jnp.pad(w, [(0, 0)] * len(lead) + [(0, 0), (0, pad - dh)])
    return w.reshape(lead + (n_heads * pad,))


def _block_diag(w):
    nb, bw, _ = w.shape
    eye = jnp.eye(nb, dtype=w.dtype)
    return jnp.einsum("ncd,nm->ncmd", w, eye).reshape(nb * bw, nb * bw)


def _project(x, p):
    B, S, D = x.shape
    H, DH, P = MLSTM_HEADS, MLSTM_DH, HEAD_PAD
    mw = H * DH
    lw = p["lru_lambda"].shape[-1]
    aw = ATT_HEADS * ATT_DH
    w_in = p["w_in"]
    offs = [0]
    for s in (mw, mw, mw, mw, H, H, lw, lw, aw, aw, aw, IDX_HEADS * IDX_DIM, IDX_DIM, IDX_HEADS):
        offs.append(offs[-1] + s)
    col = lambda k: w_in[:, offs[k]:offs[k + 1]]
    mq, mk, mv, mo, mi, mf, lx, lg, aq, ak, av, iq, ik, iw = [col(k) for k in range(14)]

    w_a = jnp.concatenate([_pad_heads(t, H, DH, P) for t in (mq, mk, mv, mo)], axis=1)
    w_l = jnp.concatenate([lx, lg], axis=1)
    w_c = jnp.concatenate([aq, ak, iq, ik, ik], axis=1)
    w_main = jnp.concatenate([w_a, w_l, w_c], axis=1).astype(BF16)
    na, nl, nc = w_a.shape[1], w_l.shape[1], w_c.shape[1]
    pad4 = lambda t: jnp.pad(t, ((0, 0), (0, 4)))
    w_t = jnp.concatenate([av, pad4(mi), pad4(mf), pad4(iw)], axis=1).T.astype(BF16)

    return _proj(x, p["norm_mix_pre"].reshape(1, D), w_main, w_t, na=na, nl=nl, nc=nc, tm=min(512, S))


def _layer_parts(x, p):
    H, DH, P = MLSTM_HEADS, MLSTM_DH, HEAD_PAD
    mw = H * DH
    lw = p["lru_lambda"].shape[-1]
    pa, plx, pc, vt, gt = _project(x, p)

    pad_col = lambda b: jnp.pad(b.reshape(H, 1), ((0, 8 - H), (0, 0)))
    ha, hb = _mixers(pa, gt, pad_col(p["b_igate"]), pad_col(p["b_fgate"]),
                     _pad_heads(p["mlstm_norm"].reshape(1, mw), H, DH, P),
                     plx, p["lru_conv_w"], p["lru_conv_b"].reshape(1, lw),
                     _block_diag(p["lru_w_a"]).astype(BF16), p["lru_b_a"].reshape(1, lw),
                     _block_diag(p["lru_w_x"]).astype(BF16), p["lru_b_x"].reshape(1, lw),
                     p["lru_lambda"].reshape(1, lw))

    hc = _dsa(pc, vt, gt)
    return ha, hb, hc


def _layer(x, p):
    B, S, D = x.shape
    H, DH, P = MLSTM_HEADS, MLSTM_DH, HEAD_PAD
    mw = H * DH
    lw = p["lru_lambda"].shape[-1]
    ha, hb, hc = _layer_parts(x, p)

    w_out = p["w_out"]
    wo_a = _pad_heads(w_out[0:mw].T, H, DH, P).T.astype(BF16)
    wo_b = w_out[mw:mw + lw].astype(BF16)
    wo_c = w_out[mw + lw:].astype(BF16)
    dff2 = p["ffn_up"].shape[1]
    return _mix_ffn(x, ha, hb, hc, wo_a, wo_b, wo_c, p["norm_mix_post"].reshape(1, D),
                    p["norm_ffn_pre"].reshape(1, D), p["ffn_up"].astype(BF16), p["ffn_conv_w"],
                    p["ffn_conv_b"].reshape(1, dff2), p["ffn_down"].astype(BF16),
                    p["norm_ffn_post"].reshape(1, D), tm=min(1024, S), tn=256, rb=256)


def kernel(x, norm_mix_pre, norm_mix_post, norm_ffn_pre, norm_ffn_post, w_in, b_igate, b_fgate,
           mlstm_norm, lru_conv_w, lru_conv_b, lru_w_a, lru_b_a, lru_w_x, lru_b_x, lru_lambda,
           w_out, ffn_up, ffn_conv_w, ffn_conv_b, ffn_down):
    params = dict(norm_mix_pre=norm_mix_pre, norm_mix_post=norm_mix_post, norm_ffn_pre=norm_ffn_pre,
                  norm_ffn_post=norm_ffn_post, w_in=w_in, b_igate=b_igate, b_fgate=b_fgate,
                  mlstm_norm=mlstm_norm, lru_conv_w=lru_conv_w, lru_conv_b=lru_conv_b,
                  lru_w_a=lru_w_a, lru_b_a=lru_b_a, lru_w_x=lru_w_x, lru_b_x=lru_b_x,
                  lru_lambda=lru_lambda, w_out=w_out, ffn_up=ffn_up, ffn_conv_w=ffn_conv_w,
                  ffn_conv_b=ffn_conv_b, ffn_down=ffn_down)
    for l in range(w_in.shape[0]):
        x = _layer(x, {k: v[l] for k, v in params.items()})
    return x
```

```python
import functools

import jax
import jax.numpy as jnp
from jax import lax
from jax.experimental import pallas as pl
from jax.experimental.pallas import tpu as pltpu

F32 = jnp.float32
BF16 = jnp.bfloat16

LANES = 128
CHUNK = 64
MLSTM_HEADS = 4
MLSTM_DH = 96
HEAD_PAD = LANES
LRU_BLOCKS = 6
LRU_CONV = 4
LRU_TILE = 64
RG_C = 8.0
ATT_HEADS = 4
ATT_DH = 64
IDX_HEADS = 4
IDX_DIM = 64
TOPK_MAX = 256
FFN_CONV = 3
EPS = 1e-6
VMEM_LIMIT = 56 * 1024 * 1024

NEG_INF = float("-inf")
INT_MIN = -2147483648
KEY_NEG_INF = -2139095040


def _cparams(sem):
    return pltpu.CompilerParams(dimension_semantics=sem, vmem_limit_bytes=VMEM_LIMIT)


def _dot(a, b):
    return jnp.dot(a, b, preferred_element_type=F32)


def _dot_nt(a, b):
    return lax.dot_general(a, b, (((1,), (1,)), ((), ())), preferred_element_type=F32)


def _split_dot(x, ones_b):
    hi = x.astype(BF16)
    r1 = x - hi.astype(F32)
    mid = r1.astype(BF16)
    lo = (r1 - mid.astype(F32)).astype(BF16)
    return _dot(hi, ones_b) + _dot(mid, ones_b) + _dot(lo, ones_b)


def _log_sigmoid(x):
    return jnp.minimum(x, 0.0) - jnp.log1p(jnp.exp(-jnp.abs(x)))


def _sigmoid(x):
    return 0.5 * jnp.tanh(0.5 * x) + 0.5


def _one_minus_sq_exp(log_a, a):
    x = 2.0 * log_a
    series = -x * (1.0 + x * (1 / 2 + x * (1 / 6 + x * (1 / 24 + x * (1 / 120)))))
    return jnp.where(x > -0.1, series, 1.0 - a * a)


def _gelu_tanh(x):
    c = 0.7978845608028654
    return 0.5 * x * (1.0 + jnp.tanh(c * (x + 0.044715 * (x * x * x))))


def _proj_body(x_ref, g_ref, w_ref, wt_ref, pa_ref, pl_ref, pc_ref, vt_ref, gt_ref,
               *, na, nl, nc):
    x = x_ref[0]
    ms = jnp.mean(x * x, axis=-1, keepdims=True)
    h = (x * lax.rsqrt(ms + EPS) * g_ref[...]).astype(BF16)
    step = 512
    for ref, base, width in ((pa_ref, 0, na), (pl_ref, na, nl), (pc_ref, na + nl, nc)):
        for lo in range(0, width, step):
            hi = min(lo + step, width)
            ref[0, :, lo:hi] = _dot(h, w_ref[:, base + lo:base + hi]).astype(ref.dtype)
    t = _dot_nt(wt_ref[...], h)
    vt_ref[0] = t[0:256].astype(BF16)
    gt_ref[0] = t[256:280]


def _proj(x, gain, w, wt, *, na, nl, nc, tm):
    B, S, D = x.shape
    nw = w.shape[1]
    grid = (B, S // tm)
    return pl.pallas_call(
        functools.partial(_proj_body, na=na, nl=nl, nc=nc),
        grid=grid,
        in_specs=[
            pl.BlockSpec((1, tm, D), lambda b, i: (b, i, 0)),
            pl.BlockSpec((1, D), lambda b, i: (0, 0)),
            pl.BlockSpec((D, nw), lambda b, i: (0, 0)),
            pl.BlockSpec((wt.shape[0], D), lambda b, i: (0, 0)),
        ],
        out_specs=[
            pl.BlockSpec((1, tm, na), lambda b, i: (b, i, 0)),
            pl.BlockSpec((1, tm, nl), lambda b, i: (b, i, 0)),
            pl.BlockSpec((1, tm, nc), lambda b, i: (b, i, 0)),
            pl.BlockSpec((1, 256, tm), lambda b, i: (b, 0, i)),
            pl.BlockSpec((1, 24, tm), lambda b, i: (b, 0, i)),
        ],
        out_shape=[
            jax.ShapeDtypeStruct((B, S, na), BF16),
            jax.ShapeDtypeStruct((B, S, nl), F32),
            jax.ShapeDtypeStruct((B, S, nc), BF16),
            jax.ShapeDtypeStruct((B, 256, S), BF16),
            jax.ShapeDtypeStruct((B, 24, S), F32),
        ],
        compiler_params=_cparams(("parallel", "parallel")),
        name="in_proj",
    )(x, gain, w, wt)


def _mixers_body(pa_ref, gt_ref, bi_ref, bf_ref, gain_ref,
                 pl_ref, cw_ref, cb_ref, wa_ref, ba_ref, wx_ref, bx_ref, lam_ref,
                 out_ref, outb_ref, c_ref, row_ref, bc_ref, rm_ref, mm_ref, *, n_pairs):
    H, P, L = MLSTM_HEADS, HEAD_PAD, CHUNK
    scale = MLSTM_DH ** -0.5
    c_ref[...] = jnp.zeros_like(c_ref)
    lru_refs = (pl_ref, cw_ref, cb_ref, wa_ref, ba_ref, wx_ref, bx_ref)
    sp = _lru_softplus_neg(lam_ref[...])

    r128 = lax.broadcasted_iota(jnp.int32, (2 * L, 2 * L), 0)
    c128 = lax.broadcasted_iota(jnp.int32, (2 * L, 2 * L), 1)
    triu_bd = jnp.where((r128 // L == c128 // L) & (r128 <= c128), 1.0, 0.0).astype(BF16)
    r64 = lax.broadcasted_iota(jnp.int32, (L, L), 0)
    c64 = lax.broadcasted_iota(jnp.int32, (L, L), 1)
    tril = c64 <= r64
    lane_p = lax.broadcasted_iota(jnp.int32, (L, P), 1)
    is_den = lane_p == MLSTM_DH
    is_feat = lane_p < MLSTM_DH
    is_last = lax.broadcasted_iota(jnp.int32, (8, L), 1) == L - 1

    def gate_step(j, carry):
        m_col, h_lru = carry
        h_lru = _lru_tile(lru_refs, outb_ref, sp, j, h_lru)
        col0 = pl.multiple_of(j * (2 * L), 2 * L)
        gi = gt_ref[0, 0:8, pl.ds(col0, 2 * L)] + bi_ref[...]
        lf = _log_sigmoid(gt_ref[0, 8:16, pl.ds(col0, 2 * L)] + bf_ref[...])
        bcum = _split_dot(lf, triu_bd)
        row_ref[0, :, pl.ds(col0, 2 * L)] = gi
        row_ref[1, :, pl.ds(col0, 2 * L)] = bcum
        for half in range(2):
            c = 2 * j + half
            row0 = pl.multiple_of(c * L, L)
            lsl = slice(half * L, (half + 1) * L)
            gi_h, bcum_h, lf_h = gi[:, lsl], bcum[:, lsl], lf[:, lsl]
            mm_ref[c] = jnp.broadcast_to(m_col, (8, P))
            b_last = jnp.sum(jnp.where(is_last, bcum_h, 0.0), axis=1, keepdims=True)
            gmax = jnp.max(b_last - bcum_h + gi_h, axis=1, keepdims=True)
            m_col = jnp.maximum(b_last + m_col, gmax)
            for h in range(H):
                lrow = jnp.broadcast_to(lf_h[h:h + 1], (L, L))
                bc = jnp.sum(jnp.where(tril, lrow, 0.0), axis=1, keepdims=True)
                d_log = jnp.where(tril, bc - bcum_h[h:h + 1] + gi_h[h:h + 1], NEG_INF)
                bc_ref[h, pl.ds(row0, L), :] = jnp.broadcast_to(bc, (L, P))
                rm_ref[h, pl.ds(row0, L), :] = jnp.broadcast_to(jnp.max(d_log, axis=1, keepdims=True), (L, P))
        return m_col, h_lru

    lru_c = cw_ref.shape[1]
    m_end, h_lru = lax.fori_loop(0, n_pairs, gate_step,
                                 (jnp.zeros((8, 1), F32), jnp.zeros((1, lru_c), F32)))
    mm_ref[2 * n_pairs] = jnp.broadcast_to(m_end, (8, P))

    def pair_step(j, h_lru):
        h_lru = _lru_tile(lru_refs, outb_ref, sp, n_pairs + j, h_lru)
        col0 = pl.multiple_of(j * (2 * L), 2 * L)
        gi = row_ref[0, :, pl.ds(col0, 2 * L)]
        bcum = row_ref[1, :, pl.ds(col0, 2 * L)]
        units = {}
        for half in range(2):
            c = 2 * j + half
            row0 = pl.multiple_of(c * L, L)
            lsl = slice(half * L, (half + 1) * L)
            gi_h, bcum_h = gi[:, lsl], bcum[:, lsl]
            m_prev_all = mm_ref[c]
            m_new_all = mm_ref[c + 1]
            for h in range(H):
                irow = gi_h[h:h + 1]
                brow = bcum_h[h:h + 1]
                m_prev = m_prev_all[h:h + 1]
                m_new = m_new_all[h:h + 1]
                bc = bc_ref[h, pl.ds(row0, L), :]
                q = pa_ref[0, pl.ds(row0, L), h * P:(h + 1) * P]
                k = pa_ref[0, pl.ds(row0, L), (H + h) * P:(H + h + 1) * P]
                v = pa_ref[0, pl.ds(row0, L), (2 * H + h) * P:(2 * H + h + 1) * P]
                v_aug = jnp.where(is_den, 1.0, v.astype(F32)).astype(BF16)
                kT = k.astype(F32).T
                b_last = bc[L - 1:L, :]
                grow = b_last[:, 0:L] - brow + irow
                wg = jnp.exp(grow - m_new[:, 0:L]) * scale
                decay = jnp.exp(b_last + m_prev - m_new)
                upd = _dot((kT * wg).astype(BF16), v_aug)
                d_log = jnp.where(tril, bc[:, 0:L] - brow + irow, NEG_INF)
                inter = bc + m_prev
                m_t = jnp.maximum(inter, rm_ref[h, pl.ds(row0, L), :])
                s = _dot(q, kT.astype(BF16)) * (jnp.exp(d_log - m_t[:, 0:L]) * scale)
                units[half, h] = (row0, q, v_aug, decay, upd, inter, m_t, s)

        for h in range(H):
            hs = slice(h * P, (h + 1) * P)
            c_state = c_ref[h]
            for half in range(2):
                row0, q, v_aug, decay, upd, inter, m_t, s = units[half, h]
                num = jnp.exp(inter - m_t) * _dot(q, c_state.astype(BF16)) + _dot(s.astype(BF16), v_aug)
                c_state = decay * c_state + upd
                den = jnp.sum(jnp.where(is_den, num, 0.0), axis=1, keepdims=True)
                feat = jnp.where(is_feat, num, 0.0)
                ssq = jnp.sum(feat * feat, axis=1, keepdims=True)
                dd = jnp.maximum(jnp.abs(den), jnp.exp(-m_t))
                inv = 1.0 / dd
                rs = inv * lax.rsqrt(ssq * (inv * inv) * (1.0 / MLSTM_DH) + EPS)
                o = pa_ref[0, pl.ds(row0, L), (3 * H + h) * P:(3 * H + h + 1) * P]
                y = feat * rs * gain_ref[:, hs] * _sigmoid(o.astype(F32))
                out_ref[0, pl.ds(row0, L), hs] = y.astype(BF16)
            c_ref[h] = c_state
        return h_lru

    lax.fori_loop(0, n_pairs, pair_step, h_lru)


def _mixers(pa, gt, bi, bf, gain, plx, cw, cb, wa, ba, wx, bx, lam):
    B, S, _ = pa.shape
    H, P = MLSTM_HEADS, HEAD_PAD
    C = lam.shape[1]
    n_pairs = S // (2 * CHUNK)
    assert S // LRU_TILE == 2 * n_pairs
    full = lambda a: pl.BlockSpec(a.shape, lambda b: (0,) * a.ndim)
    return pl.pallas_call(
        functools.partial(_mixers_body, n_pairs=n_pairs),
        grid=(B,),
        in_specs=[
            pl.BlockSpec((1, S, 4 * H * P), lambda b: (b, 0, 0)),
            pl.BlockSpec((1, 24, S), lambda b: (b, 0, 0)),
            full(bi), full(bf), full(gain),
            pl.BlockSpec((1, S, 2 * C), lambda b: (b, 0, 0)),
            full(cw), full(cb), full(wa), full(ba), full(wx), full(bx), full(lam),
        ],
        out_specs=[pl.BlockSpec((1, S, H * P), lambda b: (b, 0, 0)),
                   pl.BlockSpec((1, S, C), lambda b: (b, 0, 0))],
        out_shape=[jax.ShapeDtypeStruct((B, S, H * P), BF16),
                   jax.ShapeDtypeStruct((B, S, C), BF16)],
        scratch_shapes=[
            pltpu.VMEM((H, P, P), F32),
            pltpu.VMEM((2, 8, S), F32),
            pltpu.VMEM((H, S, P), F32),
            pltpu.VMEM((H, S, P), F32),
            pltpu.VMEM((2 * n_pairs + 1, 8, P), F32),
        ],
        compiler_params=_cparams(("parallel",)),
        name="mixers_ab",
    )(pa, gt, bi, bf, gain, plx, cw, cb, wa, ba, wx, bx, lam)


def _lru_softplus_neg(lam):
    neg_lam = -lam
    return jnp.maximum(neg_lam, 0.0) + jnp.log1p(jnp.exp(-jnp.abs(neg_lam)))


def _lru_tile(lru_refs, out_ref, sp, t, h_last):
    pl_ref, cw_ref, cb_ref, wa_ref, ba_ref, wx_ref, bx_ref = lru_refs
    tr = LRU_TILE
    C = cw_ref.shape[1]
    sub = lax.broadcasted_iota(jnp.int32, (tr, C), 0) & 7
    r0 = pl.multiple_of(t * tr, tr)
    x = pl_ref[0, pl.ds(r0, tr), 0:C]
    lg = pl_ref[0, pl.ds(r0, tr), C:2 * C]
    rp = pl.multiple_of(jnp.maximum(r0 - 8, 0), 8)
    xp = pl_ref[0, pl.ds(rp, 8), 0:C] * jnp.where(t > 0, 1.0, 0.0)
    cat = jnp.concatenate([xp, x], axis=0)
    xc = x * cw_ref[LRU_CONV - 1:LRU_CONV, :] + cb_ref[...]
    for d in range(1, LRU_CONV):
        xs = pltpu.roll(cat, d, axis=0)[8:8 + tr]
        xc = xc + xs * cw_ref[LRU_CONV - 1 - d:LRU_CONV - d, :]
    xcb = xc.astype(BF16)
    r = _sigmoid(_dot(xcb, wa_ref[...]) + ba_ref[...])
    i = _sigmoid(_dot(xcb, wx_ref[...]) + bx_ref[...])
    log_a = -RG_C * r * sp
    a = jnp.exp(log_a)
    u = jnp.sqrt(_one_minus_sq_exp(log_a, a)) * (i * xc)
    for d in (1, 2, 4):
        a_s = pltpu.roll(a, d, axis=0)
        u_s = pltpu.roll(u, d, axis=0)
        ok = sub >= d
        u = jnp.where(ok, u + a * u_s, u)
        a = jnp.where(ok, a * a_s, a)
    outs = []
    carry = h_last
    for g in range(tr // 8):
        hg = u[g * 8:(g + 1) * 8] + a[g * 8:(g + 1) * 8] * carry
        carry = hg[7:8]
        outs.append(hg)
    hfull = jnp.concatenate(outs, axis=0)
    out_ref[0, pl.ds(r0, tr), :] = (hfull * _gelu_tanh(lg)).astype(BF16)
    return carry


def _col_reduce(x, op):
    n, w = x.shape
    g = 64 if n % 64 == 0 else 8
    part = op(x.reshape(n // g, g, w), axis=0)
    return op(part, axis=0, keepdims=True)


def _head_pair_block(t):
    lane = lax.broadcasted_iota(jnp.int32, t.shape, 1)
    zero = jnp.zeros_like(t)
    return jnp.concatenate([jnp.where(lane < 64, t, zero), jnp.where(lane >= 64, t, zero)], axis=0)


DSA_SLAB = 256
DSA_QT = 256


def _dsa_body(pc_ref, vt_ref, gt_ref, out_ref, key_ref, bias_ref, l_ref, e_ref, *, nk, n_sel, q0):
    QT = DSA_QT
    idx_scale = (IDX_HEADS ** -0.5) * (IDX_DIM ** -0.5)
    att_scale = ATT_DH ** -0.5

    aq_t = pc_ref[0, q0:q0 + QT, 0:256]
    iq_t = pc_ref[0, q0:q0 + QT, 512:768]
    iw = gt_ref[0, 16:24, q0:q0 + QT] * idx_scale
    qlane = lax.broadcasted_iota(jnp.int32, (1, QT), 1)
    lim = ((q0 + qlane) // CHUNK + 1) * CHUNK
    slabs = [(r0, min(DSA_SLAB, nk - r0)) for r0 in range(0, nk, DSA_SLAB)]

    iblk = [_head_pair_block(iq_t[:, hp * 128:(hp + 1) * 128]) for hp in range(IDX_HEADS // 2)]
    for r0, n in slabs:
        ik2 = pc_ref[0, r0:r0 + n, 768:896]
        score = jnp.zeros((n, QT), F32)
        for hp in range(IDX_HEADS // 2):
            r = _dot_nt(ik2, iblk[hp])
            score = score + jnp.maximum(r[:, 0:QT], 0.0) * iw[2 * hp:2 * hp + 1, :]
            score = score + jnp.maximum(r[:, QT:2 * QT], 0.0) * iw[2 * hp + 1:2 * hp + 2, :]
        krow = r0 + lax.broadcasted_iota(jnp.int32, (n, QT), 0)
        score = jnp.where(krow < lim, score, NEG_INF)
        bits = pltpu.bitcast(score, jnp.int32)
        key_ref[r0:r0 + n, :] = jnp.where(bits >= 0, bits, jnp.int32(INT_MIN) - bits)

    def bit_step(i, carry):
        t, c_t = carry
        cand = t + lax.shift_left(jnp.int32(1), 31 - i)
        acc = jnp.zeros((64, QT), F32)
        for r0 in range(0, nk, 64):
            acc = acc + jnp.where(key_ref[r0:r0 + 64, :] >= cand, 1.0, 0.0)
        cnt = jnp.sum(acc, axis=0, keepdims=True)
        ok = cnt >= n_sel
        return jnp.where(ok, cand, t), jnp.where(ok, cnt, c_t)

    tau, cnt_ge = lax.fori_loop(0, 32, bit_step,
                                (jnp.full((1, QT), INT_MIN, jnp.int32), jnp.full((1, QT), float(nk), F32)))
    valid = tau > KEY_NEG_INF
    tau_t = jnp.maximum(tau, KEY_NEG_INF)
    drop = jnp.where(valid, cnt_ge - n_sel, float(nk))
    tri_r = lax.broadcasted_iota(jnp.int32, (DSA_SLAB, DSA_SLAB), 0)
    tri_c = lax.broadcasted_iota(jnp.int32, (DSA_SLAB, DSA_SLAB), 1)
    tri = jnp.where(tri_c >= tri_r, 1.0, 0.0).astype(BF16)
    later = jnp.zeros((1, QT), F32)
    for r0, n in reversed(slabs):
        key = key_ref[r0:r0 + n, :]
        eq = jnp.where(key == tau_t, 1.0, 0.0)
        rank_end = _dot(tri[0:n, 0:n], eq.astype(BF16)) + later
        later = later + jnp.sum(eq, axis=0, keepdims=True)
        tie_bias = jnp.where(rank_end > drop, 0.0, NEG_INF)
        bias_ref[r0:r0 + n, :] = jnp.where(key > tau_t, 0.0, jnp.where(key == tau_t, tie_bias, NEG_INF))

    outs = []
    for hp in range(ATT_HEADS // 2):
        q2 = aq_t[:, hp * 128:(hp + 1) * 128] * jnp.asarray(att_scale, BF16)
        blk = _head_pair_block(q2)
        mx = [jnp.full((64, QT), NEG_INF, F32) for _ in range(2)]
        for r0, n in slabs:
            lg = _dot_nt(pc_ref[0, r0:r0 + n, 256 + hp * 128:256 + (hp + 1) * 128], blk)
            b = bias_ref[r0:r0 + n, :]
            for sub in range(2):
                l = lg[:, sub * QT:(sub + 1) * QT] + b
                l_ref[sub, r0:r0 + n, :] = l
                mx[sub] = jnp.maximum(mx[sub], jnp.max(l.reshape(n // 64, 64, QT), axis=0))
        for sub in range(2):
            h = 2 * hp + sub
            m = jnp.max(mx[sub], axis=0, keepdims=True)
            den = jnp.zeros((64, QT), F32)
            for r0, n in slabs:
                e = jnp.exp(l_ref[sub, r0:r0 + n, :] - m)
                den = den + jnp.sum(e.reshape(n // 64, 64, QT), axis=0)
                e_ref[r0:r0 + n, :] = e.astype(BF16)
            o_t = _dot(vt_ref[0, h * ATT_DH:(h + 1) * ATT_DH, 0:nk], e_ref[...])
            outs.append(o_t / jnp.sum(den, axis=0, keepdims=True))
    out_t = jnp.concatenate(outs, axis=0)
    out_ref[0] = out_t.T.astype(BF16)


def _dsa_tile(pc, vt, gt, *, nk, n_sel):
    B, S, ncols = pc.shape
    QT = DSA_QT
    return pl.pallas_call(
        functools.partial(_dsa_body, nk=nk, n_sel=n_sel, q0=nk - QT),
        grid=(B,),
        in_specs=[
            pl.BlockSpec((1, nk, ncols), lambda b: (b, 0, 0)),
            pl.BlockSpec((1, 256, nk), lambda b: (b, 0, 0)),
            pl.BlockSpec((1, 24, nk), lambda b: (b, 0, 0)),
        ],
        out_specs=pl.BlockSpec((1, QT, 256), lambda b: (b, 0, 0)),
        out_shape=jax.ShapeDtypeStruct((B, QT, 256), BF16),
        scratch_shapes=[
            pltpu.VMEM((nk, QT), jnp.int32),
            pltpu.VMEM((nk, QT), F32),
            pltpu.VMEM((2, nk, QT), F32),
            pltpu.VMEM((nk, QT), BF16),
        ],
        compiler_params=_cparams(("parallel",)),
        name=f"dsa_{nk}",
    )(pc, vt, gt)


def _dsa(pc, vt, gt):
    B, S, _ = pc.shape
    n_sel = min(TOPK_MAX, S // 4)
    outs = [_dsa_tile(pc, vt, gt, nk=nk, n_sel=n_sel) for nk in range(DSA_QT, S + 1, DSA_QT)]
    return jnp.concatenate(outs, axis=1)


def _ffn_body(x_ref, ha_ref, hb_ref, hc_ref, woa_ref, wob_ref, woc_ref, gmix_ref,
              gpre_ref, wup_ref, cw_ref, cb_ref, wd_ref, gpost_ref, o_ref, h_ref, act_ref, halo_ref,
              *, tm, dff, tn, rb):
    i = pl.program_id(1)

    @pl.when(i == 0)
    def _():
        halo_ref[...] = jnp.zeros_like(halo_ref)

    def row_block(r, carry):
        r0 = pl.multiple_of(r * rb, rb)
        rows = pl.ds(r0, rb)
        mix = (_dot(ha_ref[0, rows, :], woa_ref[...]) + _dot(hb_ref[0, rows, :], wob_ref[...])
               + _dot(hc_ref[0, rows, :], woc_ref[...]))
        ms_mix = jnp.mean(mix * mix, axis=-1, keepdims=True)
        x1 = x_ref[0, rows, :] + mix * lax.rsqrt(ms_mix + EPS) * gmix_ref[...]
        o_ref[0, rows, :] = x1
        ms = jnp.mean(x1 * x1, axis=-1, keepdims=True)
        h_ref[...] = (x1 * lax.rsqrt(ms + EPS) * gpre_ref[...]).astype(BF16)

        def conv(col0, slot, c0):
            xu = _dot(h_ref[...], wup_ref[:, col0:col0 + tn])
            prev = halo_ref[slot, :, c0:c0 + tn]
            halo_ref[slot, :, c0:c0 + tn] = xu[rb - 8:rb]
            cat = jnp.concatenate([prev, xu], axis=0)
            y = xu * cw_ref[FFN_CONV - 1:FFN_CONV, col0:col0 + tn] + cb_ref[:, col0:col0 + tn]
            for d in range(1, FFN_CONV):
                y = y + pltpu.roll(cat, d, axis=0)[8:8 + rb] * cw_ref[FFN_CONV - 1 - d:FFN_CONV - d, col0:col0 + tn]
            return y

        for c0 in range(0, dff, tn):
            gate = conv(c0, 0, c0)
            up = conv(dff + c0, 1, c0)
            act_ref[:, c0:c0 + tn] = (_gelu_tanh(gate) * up).astype(BF16)
        y = _dot(act_ref[...], wd_ref[...])
        ms_y = jnp.mean(y * y, axis=-1, keepdims=True)
        o_ref[0, rows, :] = o_ref[0, rows, :] + y * lax.rsqrt(ms_y + EPS) * gpost_ref[...]
        return carry

    lax.fori_loop(0, tm // rb, row_block, 0)


def _mix_ffn(x, ha, hb, hc, wo_a, wo_b, wo_c, gmix, gpre, w_up, cw, cb, w_down, gpost, *, tm, tn, rb):
    B, S, D = x.shape
    dff = w_down.shape[0]
    row = lambda a: pl.BlockSpec((1, tm, a.shape[2]), lambda b, i: (b, i, 0))
    resident = lambda a: pl.BlockSpec(a.shape, lambda b, i: (0, 0), pipeline_mode=pl.Buffered(1))
    weights = (wo_a, wo_b, wo_c, gmix, gpre, w_up, cw, cb, w_down, gpost)
    return pl.pallas_call(
        functools.partial(_ffn_body, tm=tm, dff=dff, tn=tn, rb=rb),
        grid=(B, S // tm),
        in_specs=[row(x), row(ha), row(hb), row(hc)] + [resident(w) for w in weights],
        out_specs=row(x),
        out_shape=jax.ShapeDtypeStruct((B, S, D), F32),
        scratch_shapes=[
            pltpu.VMEM((rb, D), BF16),
            pltpu.VMEM((rb, dff), BF16),
            pltpu.VMEM((2, 8, dff), F32),
        ],
        compiler_params=_cparams(("parallel", "arbitrary")),
        name="mix_ffn",
    )(x, ha, hb, hc, *weights)


def _pad_heads(w, n_heads, dh, pad):
    lead = w.shape[:-1]
    w = w.reshape(lead + (n_heads, dh))
    w = jnp.pad(w, [(0, 0)] * len(lead) + [(0, 0), (0, pad - dh)])
    return w.reshape(lead + (n_heads * pad,))


def _block_diag(w):
    nb, bw, _ = w.shape
    eye = jnp.eye(nb, dtype=w.dtype)
    return jnp.einsum("ncd,nm->ncmd", w, eye).reshape(nb * bw, nb * bw)


def _project(x, p):
    B, S, D = x.shape
    H, DH, P = MLSTM_HEADS, MLSTM_DH, HEAD_PAD
    mw = H * DH
    lw = p["lru_lambda"].shape[-1]
    aw = ATT_HEADS * ATT_DH
    w_in = p["w_in"]
    offs = [0]
    for s in (mw, mw, mw, mw, H, H, lw, lw, aw, aw, aw, IDX_HEADS * IDX_DIM, IDX_DIM, IDX_HEADS):
        offs.append(offs[-1] + s)
    col = lambda k: w_in[:, offs[k]:offs[k + 1]]
    mq, mk, mv, mo, mi, mf, lx, lg, aq, ak, av, iq, ik, iw = [col(k) for k in range(14)]

    w_a = jnp.concatenate([_pad_heads(t, H, DH, P) for t in (mq, mk, mv, mo)], axis=1)
    w_l = jnp.concatenate([lx, lg], axis=1)
    w_c = jnp.concatenate([aq, ak, iq, ik, ik], axis=1)
    w_main = jnp.concatenate([w_a, w_l, w_c], axis=1).astype(BF16)
    na, nl, nc = w_a.shape[1], w_l.shape[1], w_c.shape[1]
    pad4 = lambda t: jnp.pad(t, ((0, 0), (0, 4)))
    w_t = jnp.concatenate([av, pad4(mi), pad4(mf), pad4(iw)], axis=1).T.astype(BF16)

    return _proj(x, p["norm_mix_pre"].reshape(1, D), w_main, w_t, na=na, nl=nl, nc=nc, tm=min(512, S))


def _layer_parts(x, p):
    H, DH, P = MLSTM_HEADS, MLSTM_DH, HEAD_PAD
    mw = H * DH
    lw = p["lru_lambda"].shape[-1]
    pa, plx, pc, vt, gt = _project(x, p)

    pad_col = lambda b: jnp.pad(b.reshape(H, 1), ((0, 8 - H), (0, 0)))
    ha, hb = _mixers(pa, gt, pad_col(p["b_igate"]), pad_col(p["b_fgate"]),
                     _pad_heads(p["mlstm_norm"].reshape(1, mw), H, DH, P),
                     plx, p["lru_conv_w"], p["lru_conv_b"].reshape(1, lw),
                     _block_diag(p["lru_w_a"]).astype(BF16), p["lru_b_a"].reshape(1, lw),
                     _block_diag(p["lru_w_x"]).astype(BF16), p["lru_b_x"].reshape(1, lw),
                     p["lru_lambda"].reshape(1, lw))

    hc = _dsa(pc, vt, gt)
    return ha, hb, hc


def _layer(x, p):
    B, S, D = x.shape
    H, DH, P = MLSTM_HEADS, MLSTM_DH, HEAD_PAD
    mw = H * DH
    lw = p["lru_lambda"].shape[-1]
    ha, hb, hc = _layer_parts(x, p)

    w_out = p["w_out"]
    wo_a = _pad_heads(w_out[0:mw].T, H, DH, P).T.astype(BF16)
    wo_b = w_out[mw:mw + lw].astype(BF16)
    wo_c = w_out[mw + lw:].astype(BF16)
    dff2 = p["ffn_up"].shape[1]
    return _mix_ffn(x, ha, hb, hc, wo_a, wo_b, wo_c, p["norm_mix_post"].reshape(1, D),
                    p["norm_ffn_pre"].reshape(1, D), p["ffn_up"].astype(BF16), p["ffn_conv_w"],
                    p["ffn_conv_b"].reshape(1, dff2), p["ffn_down"].astype(BF16),
                    p["norm_ffn_post"].reshape(1, D), tm=min(1024, S), tn=256, rb=512)


def kernel(x, norm_mix_pre, norm_mix_post, norm_ffn_pre, norm_ffn_post, w_in, b_igate, b_fgate,
           mlstm_norm, lru_conv_w, lru_conv_b, lru_w_a, lru_b_a, lru_w_x, lru_b_x, lru_lambda,
           w_out, ffn_up, ffn_conv_w, ffn_conv_b, ffn_down):
    params = dict(norm_mix_pre=norm_mix_pre, norm_mix_post=norm_mix_post, norm_ffn_pre=norm_ffn_pre,
                  norm_ffn_post=norm_ffn_post, w_in=w_in, b_igate=b_igate, b_fgate=b_fgate,
                  mlstm_norm=mlstm_norm, lru_conv_w=lru_conv_w, lru_conv_b=lru_conv_b,
                  lru_w_a=lru_w_a, lru_b_a=lru_b_a, lru_w_x=lru_w_x, lru_b_x=lru_b_x,
                  lru_lambda=lru_lambda, w_out=w_out, ffn_up=ffn_up, ffn_conv_w=ffn_conv_w,
                  ffn_conv_b=ffn_conv_b, ffn_down=ffn_down)
    for l in range(w_in.shape[0]):
        x = _layer(x, {k: v[l] for k, v in params.items()})
    return x
```

```python
import functools

import jax
import jax.numpy as jnp
from jax import lax
from jax.experimental import pallas as pl
from jax.experimental.pallas import tpu as pltpu

F32 = jnp.float32
BF16 = jnp.bfloat16

LANES = 128
CHUNK = 64
MLSTM_HEADS = 4
MLSTM_DH = 96
HEAD_PAD = LANES
LRU_BLOCKS = 6
LRU_CONV = 4
LRU_TILE = 64
RG_C = 8.0
ATT_HEADS = 4
ATT_DH = 64
IDX_HEADS = 4
IDX_DIM = 64
TOPK_MAX = 256
FFN_CONV = 3
EPS = 1e-6
VMEM_LIMIT = 56 * 1024 * 1024

NEG_INF = float("-inf")
INT_MIN = -2147483648
KEY_NEG_INF = -2139095040


def _cparams(sem):
    return pltpu.CompilerParams(dimension_semantics=sem, vmem_limit_bytes=VMEM_LIMIT)


def _dot(a, b):
    return jnp.dot(a, b, preferred_element_type=F32)


def _dot_nt(a, b):
    return lax.dot_general(a, b, (((1,), (1,)), ((), ())), preferred_element_type=F32)


def _split_dot(x, ones_b):
    hi = x.astype(BF16)
    r1 = x - hi.astype(F32)
    mid = r1.astype(BF16)
    lo = (r1 - mid.astype(F32)).astype(BF16)
    return _dot(hi, ones_b) + _dot(mid, ones_b) + _dot(lo, ones_b)


def _log_sigmoid(x):
    return jnp.minimum(x, 0.0) - jnp.log1p(jnp.exp(-jnp.abs(x)))


def _sigmoid(x):
    return 0.5 * jnp.tanh(0.5 * x) + 0.5


def _one_minus_sq_exp(log_a, a):
    x = 2.0 * log_a
    series = -x * (1.0 + x * (1 / 2 + x * (1 / 6 + x * (1 / 24 + x * (1 / 120)))))
    return jnp.where(x > -0.1, series, 1.0 - a * a)


def _gelu_tanh(x):
    c = 0.7978845608028654
    return 0.5 * x * (1.0 + jnp.tanh(c * (x + 0.044715 * (x * x * x))))


def _proj_body(x_ref, g_ref, w_ref, wt_ref, pa_ref, pl_ref, pc_ref, vt_ref, gt_ref,
               *, na, nl, nc):
    x = x_ref[0]
    ms = jnp.mean(x * x, axis=-1, keepdims=True)
    h = (x * lax.rsqrt(ms + EPS) * g_ref[...]).astype(BF16)
    step = 512
    for ref, base, width in ((pa_ref, 0, na), (pl_ref, na, nl), (pc_ref, na + nl, nc)):
        for lo in range(0, width, step):
            hi = min(lo + step, width)
            ref[0, :, lo:hi] = _dot(h, w_ref[:, base + lo:base + hi]).astype(ref.dtype)
    t = _dot_nt(wt_ref[...], h)
    vt_ref[0] = t[0:256].astype(BF16)
    gt_ref[0] = t[256:280]


def _layer_spec(a, l, **kw):
    zeros = (0,) * (a.ndim - 1)
    return pl.BlockSpec((None,) + a.shape[1:], lambda *_: (l,) + zeros, **kw)


def _proj(x, gain, w, wt, l, *, na, nl, nc, tm):
    B, S, D = x.shape
    grid = (B, S // tm)
    return pl.pallas_call(
        functools.partial(_proj_body, na=na, nl=nl, nc=nc),
        grid=grid,
        in_specs=[
            pl.BlockSpec((1, tm, D), lambda b, i: (b, i, 0)),
            _layer_spec(gain, l), _layer_spec(w, l), _layer_spec(wt, l),
        ],
        out_specs=[
            pl.BlockSpec((1, tm, na), lambda b, i: (b, i, 0)),
            pl.BlockSpec((1, tm, nl), lambda b, i: (b, i, 0)),
            pl.BlockSpec((1, tm, nc), lambda b, i: (b, i, 0)),
            pl.BlockSpec((1, 256, tm), lambda b, i: (b, 0, i)),
            pl.BlockSpec((1, 24, tm), lambda b, i: (b, 0, i)),
        ],
        out_shape=[
            jax.ShapeDtypeStruct((B, S, na), BF16),
            jax.ShapeDtypeStruct((B, S, nl), F32),
            jax.ShapeDtypeStruct((B, S, nc), BF16),
            jax.ShapeDtypeStruct((B, 256, S), BF16),
            jax.ShapeDtypeStruct((B, 24, S), F32),
        ],
        compiler_params=_cparams(("parallel", "parallel")),
        name="in_proj",
    )(x, gain, w, wt)


def _mixers_body(pa_ref, gt_ref, bi_ref, bf_ref, gain_ref,
                 pl_ref, cw_ref, cb_ref, wa_ref, ba_ref, wx_ref, bx_ref, lam_ref,
                 out_ref, outb_ref, c_ref, row_ref, bc_ref, rm_ref, mm_ref, *, n_pairs):
    H, P, L = MLSTM_HEADS, HEAD_PAD, CHUNK
    scale = MLSTM_DH ** -0.5
    c_ref[...] = jnp.zeros_like(c_ref)
    lru_refs = (pl_ref, cw_ref, cb_ref, wa_ref, ba_ref, wx_ref, bx_ref)
    sp = _lru_softplus_neg(lam_ref[...])

    r128 = lax.broadcasted_iota(jnp.int32, (2 * L, 2 * L), 0)
    c128 = lax.broadcasted_iota(jnp.int32, (2 * L, 2 * L), 1)
    triu_bd = jnp.where((r128 // L == c128 // L) & (r128 <= c128), 1.0, 0.0).astype(BF16)
    r64 = lax.broadcasted_iota(jnp.int32, (L, L), 0)
    c64 = lax.broadcasted_iota(jnp.int32, (L, L), 1)
    tril = c64 <= r64
    lane_p = lax.broadcasted_iota(jnp.int32, (L, P), 1)
    is_den = lane_p == MLSTM_DH
    is_feat = lane_p < MLSTM_DH
    is_last = lax.broadcasted_iota(jnp.int32, (8, L), 1) == L - 1

    def gate_step(j, carry):
        m_col, h_lru = carry
        h_lru = _lru_tile(lru_refs, outb_ref, sp, j, h_lru)
        col0 = pl.multiple_of(j * (2 * L), 2 * L)
        gi = gt_ref[0, 0:8, pl.ds(col0, 2 * L)] + bi_ref[...]
        lf = _log_sigmoid(gt_ref[0, 8:16, pl.ds(col0, 2 * L)] + bf_ref[...])
        bcum = _split_dot(lf, triu_bd)
        row_ref[0, :, pl.ds(col0, 2 * L)] = gi
        row_ref[1, :, pl.ds(col0, 2 * L)] = bcum
        for half in range(2):
            c = 2 * j + half
            row0 = pl.multiple_of(c * L, L)
            lsl = slice(half * L, (half + 1) * L)
            gi_h, bcum_h, lf_h = gi[:, lsl], bcum[:, lsl], lf[:, lsl]
            mm_ref[c] = jnp.broadcast_to(m_col, (8, P))
            b_last = jnp.sum(jnp.where(is_last, bcum_h, 0.0), axis=1, keepdims=True)
            gmax = jnp.max(b_last - bcum_h + gi_h, axis=1, keepdims=True)
            m_col = jnp.maximum(b_last + m_col, gmax)
            for h in range(H):
                lrow = jnp.broadcast_to(lf_h[h:h + 1], (L, L))
                bc = jnp.sum(jnp.where(tril, lrow, 0.0), axis=1, keepdims=True)
                d_log = jnp.where(tril, bc - bcum_h[h:h + 1] + gi_h[h:h + 1], NEG_INF)
                bc_ref[h, pl.ds(row0, L), :] = jnp.broadcast_to(bc, (L, P))
                rm_ref[h, pl.ds(row0, L), :] = jnp.broadcast_to(jnp.max(d_log, axis=1, keepdims=True), (L, P))
        return m_col, h_lru

    lru_c = cw_ref.shape[1]
    m_end, h_lru = lax.fori_loop(0, n_pairs, gate_step,
                                 (jnp.zeros((8, 1), F32), jnp.zeros((1, lru_c), F32)))
    mm_ref[2 * n_pairs] = jnp.broadcast_to(m_end, (8, P))

    def pair_step(j, h_lru):
        h_lru = _lru_tile(lru_refs, outb_ref, sp, n_pairs + j, h_lru)
        col0 = pl.multiple_of(j * (2 * L), 2 * L)
        gi = row_ref[0, :, pl.ds(col0, 2 * L)]
        bcum = row_ref[1, :, pl.ds(col0, 2 * L)]
        units = {}
        for half in range(2):
            c = 2 * j + half
            row0 = pl.multiple_of(c * L, L)
            lsl = slice(half * L, (half + 1) * L)
            gi_h, bcum_h = gi[:, lsl], bcum[:, lsl]
            m_prev_all = mm_ref[c]
            m_new_all = mm_ref[c + 1]
            for h in range(H):
                irow = gi_h[h:h + 1]
                brow = bcum_h[h:h + 1]
                m_prev = m_prev_all[h:h + 1]
                m_new = m_new_all[h:h + 1]
                bc = bc_ref[h, pl.ds(row0, L), :]
                q = pa_ref[0, pl.ds(row0, L), h * P:(h + 1) * P]
                k = pa_ref[0, pl.ds(row0, L), (H + h) * P:(H + h + 1) * P]
                v = pa_ref[0, pl.ds(row0, L), (2 * H + h) * P:(2 * H + h + 1) * P]
                v_aug = jnp.where(is_den, 1.0, v.astype(F32)).astype(BF16)
                kT = k.astype(F32).T
                b_last = bc[L - 1:L, :]
                grow = b_last[:, 0:L] - brow + irow
                wg = jnp.exp(grow - m_new[:, 0:L]) * scale
                decay = jnp.exp(b_last + m_prev - m_new)
                upd = _dot((kT * wg).astype(BF16), v_aug)
                d_log = jnp.where(tril, bc[:, 0:L] - brow + irow, NEG_INF)
                inter = bc + m_prev
                m_t = jnp.maximum(inter, rm_ref[h, pl.ds(row0, L), :])
                s = _dot(q, kT.astype(BF16)) * (jnp.exp(d_log - m_t[:, 0:L]) * scale)
                units[half, h] = (row0, q, v_aug, decay, upd, inter, m_t, s)

        for h in range(H):
            hs = slice(h * P, (h + 1) * P)
            c_state = c_ref[h]
            for half in range(2):
                row0, q, v_aug, decay, upd, inter, m_t, s = units[half, h]
                num = jnp.exp(inter - m_t) * _dot(q, c_state.astype(BF16)) + _dot(s.astype(BF16), v_aug)
                c_state = decay * c_state + upd
                den = jnp.sum(jnp.where(is_den, num, 0.0), axis=1, keepdims=True)
                feat = jnp.where(is_feat, num, 0.0)
                ssq = jnp.sum(feat * feat, axis=1, keepdims=True)
                dd = jnp.maximum(jnp.abs(den), jnp.exp(-m_t))
                inv = 1.0 / dd
                rs = inv * lax.rsqrt(ssq * (inv * inv) * (1.0 / MLSTM_DH) + EPS)
                o = pa_ref[0, pl.ds(row0, L), (3 * H + h) * P:(3 * H + h + 1) * P]
                y = feat * rs * gain_ref[:, hs] * _sigmoid(o.astype(F32))
                out_ref[0, pl.ds(row0, L), hs] = y.astype(BF16)
            c_ref[h] = c_state
        return h_lru

    lax.fori_loop(0, n_pairs, pair_step, h_lru)


def _mixers(pa, gt, bi, bf, gain, plx, cw, cb, wa, ba, wx, bx, lam, l):
    B, S, _ = pa.shape
    H, P = MLSTM_HEADS, HEAD_PAD
    C = lam.shape[-1]
    n_pairs = S // (2 * CHUNK)
    assert S // LRU_TILE == 2 * n_pairs
    full = lambda a: _layer_spec(a, l)
    return pl.pallas_call(
        functools.partial(_mixers_body, n_pairs=n_pairs),
        grid=(B,),
        in_specs=[
            pl.BlockSpec((1, S, 4 * H * P), lambda b: (b, 0, 0)),
            pl.BlockSpec((1, 24, S), lambda b: (b, 0, 0)),
            full(bi), full(bf), full(gain),
            pl.BlockSpec((1, S, 2 * C), lambda b: (b, 0, 0)),
            full(cw), full(cb), full(wa), full(ba), full(wx), full(bx), full(lam),
        ],
        out_specs=[pl.BlockSpec((1, S, H * P), lambda b: (b, 0, 0)),
                   pl.BlockSpec((1, S, C), lambda b: (b, 0, 0))],
        out_shape=[jax.ShapeDtypeStruct((B, S, H * P), BF16),
                   jax.ShapeDtypeStruct((B, S, C), BF16)],
        scratch_shapes=[
            pltpu.VMEM((H, P, P), F32),
            pltpu.VMEM((2, 8, S), F32),
            pltpu.VMEM((H, S, P), F32),
            pltpu.VMEM((H, S, P), F32),
            pltpu.VMEM((2 * n_pairs + 1, 8, P), F32),
        ],
        compiler_params=_cparams(("parallel",)),
        name="mixers_ab",
    )(pa, gt, bi, bf, gain, plx, cw, cb, wa, ba, wx, bx, lam)


def _lru_softplus_neg(lam):
    neg_lam = -lam
    return jnp.maximum(neg_lam, 0.0) + jnp.log1p(jnp.exp(-jnp.abs(neg_lam)))


def _lru_tile(lru_refs, out_ref, sp, t, h_last):
    pl_ref, cw_ref, cb_ref, wa_ref, ba_ref, wx_ref, bx_ref = lru_refs
    tr = LRU_TILE
    C = cw_ref.shape[1]
    sub = lax.broadcasted_iota(jnp.int32, (tr, C), 0) & 7
    r0 = pl.multiple_of(t * tr, tr)
    x = pl_ref[0, pl.ds(r0, tr), 0:C]
    lg = pl_ref[0, pl.ds(r0, tr), C:2 * C]
    rp = pl.multiple_of(jnp.maximum(r0 - 8, 0), 8)
    xp = pl_ref[0, pl.ds(rp, 8), 0:C] * jnp.where(t > 0, 1.0, 0.0)
    cat = jnp.concatenate([xp, x], axis=0)
    xc = x * cw_ref[LRU_CONV - 1:LRU_CONV, :] + cb_ref[...]
    for d in range(1, LRU_CONV):
        xs = pltpu.roll(cat, d, axis=0)[8:8 + tr]
        xc = xc + xs * cw_ref[LRU_CONV - 1 - d:LRU_CONV - d, :]
    xcb = xc.astype(BF16)
    r = _sigmoid(_dot(xcb, wa_ref[...]) + ba_ref[...])
    i = _sigmoid(_dot(xcb, wx_ref[...]) + bx_ref[...])
    log_a = -RG_C * r * sp
    a = jnp.exp(log_a)
    u = jnp.sqrt(_one_minus_sq_exp(log_a, a)) * (i * xc)
    for d in (1, 2, 4):
        a_s = pltpu.roll(a, d, axis=0)
        u_s = pltpu.roll(u, d, axis=0)
        ok = sub >= d
        u = jnp.where(ok, u + a * u_s, u)
        a = jnp.where(ok, a * a_s, a)
    outs = []
    carry = h_last
    for g in range(tr // 8):
        hg = u[g * 8:(g + 1) * 8] + a[g * 8:(g + 1) * 8] * carry
        carry = hg[7:8]
        outs.append(hg)
    hfull = jnp.concatenate(outs, axis=0)
    out_ref[0, pl.ds(r0, tr), :] = (hfull * _gelu_tanh(lg)).astype(BF16)
    return carry


def _col_reduce(x, op):
    n, w = x.shape
    g = 64 if n % 64 == 0 else 8
    part = op(x.reshape(n // g, g, w), axis=0)
    return op(part, axis=0, keepdims=True)


def _head_pair_block(t):
    lane = lax.broadcasted_iota(jnp.int32, t.shape, 1)
    zero = jnp.zeros_like(t)
    return jnp.concatenate([jnp.where(lane < 64, t, zero), jnp.where(lane >= 64, t, zero)], axis=0)


DSA_SLAB = 256
DSA_QT = 256


def _dsa_body(pc_ref, vt_ref, gt_ref, hc_in_ref, out_ref, key_ref, bias_ref, l_ref, e_ref, *, nk, n_sel, q0):
    del hc_in_ref
    QT = DSA_QT
    idx_scale = (IDX_HEADS ** -0.5) * (IDX_DIM ** -0.5)
    att_scale = ATT_DH ** -0.5

    aq_t = pc_ref[0, q0:q0 + QT, 0:256]
    iq_t = pc_ref[0, q0:q0 + QT, 512:768]
    iw = gt_ref[0, 16:24, q0:q0 + QT] * idx_scale
    qlane = lax.broadcasted_iota(jnp.int32, (1, QT), 1)
    lim = ((q0 + qlane) // CHUNK + 1) * CHUNK
    slabs = [(r0, min(DSA_SLAB, nk - r0)) for r0 in range(0, nk, DSA_SLAB)]

    iblk = [_head_pair_block(iq_t[:, hp * 128:(hp + 1) * 128]) for hp in range(IDX_HEADS // 2)]
    for r0, n in slabs:
        ik2 = pc_ref[0, r0:r0 + n, 768:896]
        score = jnp.zeros((n, QT), F32)
        for hp in range(IDX_HEADS // 2):
            r = _dot_nt(ik2, iblk[hp])
            score = score + jnp.maximum(r[:, 0:QT], 0.0) * iw[2 * hp:2 * hp + 1, :]
            score = score + jnp.maximum(r[:, QT:2 * QT], 0.0) * iw[2 * hp + 1:2 * hp + 2, :]
        if r0 + n > q0 + CHUNK:
            krow = r0 + lax.broadcasted_iota(jnp.int32, (n, QT), 0)
            score = jnp.where(krow < lim, score, NEG_INF)
        bits = pltpu.bitcast(score, jnp.int32)
        key_ref[r0:r0 + n, :] = jnp.where(bits >= 0, bits, jnp.int32(INT_MIN) - bits)

    def bit_step(i, carry):
        t, c_t = carry
        cand = t + lax.shift_left(jnp.int32(1), 31 - i)
        acc = jnp.zeros((64, QT), F32)
        for r0 in range(0, nk, 64):
            acc = acc + jnp.where(key_ref[r0:r0 + 64, :] >= cand, 1.0, 0.0)
        cnt = jnp.sum(acc, axis=0, keepdims=True)
        ok = cnt >= n_sel
        return jnp.where(ok, cand, t), jnp.where(ok, cnt, c_t)

    tau, cnt_ge = jnp.full((1, QT), INT_MIN, jnp.int32), jnp.full((1, QT), float(nk), F32)
    if nk > n_sel:
        tau, cnt_ge = lax.fori_loop(0, 32, bit_step, (tau, cnt_ge))
    valid = tau > KEY_NEG_INF
    tau_t = jnp.maximum(tau, KEY_NEG_INF)
    drop = jnp.where(valid, cnt_ge - n_sel, float(nk))
    tri_r = lax.broadcasted_iota(jnp.int32, (DSA_SLAB, DSA_SLAB), 0)
    tri_c = lax.broadcasted_iota(jnp.int32, (DSA_SLAB, DSA_SLAB), 1)
    tri = jnp.where(tri_c >= tri_r, 1.0, 0.0).astype(BF16)
    later = jnp.zeros((1, QT), F32)
    for r0, n in reversed(slabs):
        key = key_ref[r0:r0 + n, :]
        eq = jnp.where(key == tau_t, 1.0, 0.0)
        rank_slab = _dot(tri[0:n, 0:n], eq.astype(BF16))
        tie_bias = jnp.where(rank_slab > drop - later, 0.0, NEG_INF)
        later = later + rank_slab[0:1, :]
        bias_ref[r0:r0 + n, :] = jnp.where(key > tau_t, 0.0, jnp.where(key == tau_t, tie_bias, NEG_INF))

    outs = []
    for hp in range(ATT_HEADS // 2):
        q2 = aq_t[:, hp * 128:(hp + 1) * 128] * jnp.asarray(att_scale, BF16)
        blk = _head_pair_block(q2)
        mx = [jnp.full((64, QT), NEG_INF, F32) for _ in range(2)]
        for r0, n in slabs:
            lg = _dot_nt(pc_ref[0, r0:r0 + n, 256 + hp * 128:256 + (hp + 1) * 128], blk)
            b = bias_ref[r0:r0 + n, :]
            for sub in range(2):
                l = lg[:, sub * QT:(sub + 1) * QT] + b
                l_ref[sub, r0:r0 + n, :] = l
                mx[sub] = jnp.maximum(mx[sub], jnp.max(l.reshape(n // 64, 64, QT), axis=0))
        for sub in range(2):
            h = 2 * hp + sub
            m = jnp.max(mx[sub], axis=0, keepdims=True)
            den = jnp.zeros((64, QT), F32)
            for r0, n in slabs:
                e = jnp.exp(l_ref[sub, r0:r0 + n, :] - m)
                den = den + jnp.sum(e.reshape(n // 64, 64, QT), axis=0)
                e_ref[r0:r0 + n, :] = e.astype(BF16)
            o_t = _dot(vt_ref[0, h * ATT_DH:(h + 1) * ATT_DH, 0:nk], e_ref[...])
            outs.append(o_t / jnp.sum(den, axis=0, keepdims=True))
    out_t = jnp.concatenate(outs, axis=0)
    out_ref[0] = out_t.T.astype(BF16)


def _dsa_tile(pc, vt, gt, hc, *, nk, n_sel):
    B, S, ncols = pc.shape
    QT = DSA_QT
    tile = nk // QT - 1
    return pl.pallas_call(
        functools.partial(_dsa_body, nk=nk, n_sel=n_sel, q0=nk - QT),
        grid=(B,),
        in_specs=[
            pl.BlockSpec((1, nk, ncols), lambda b: (b, 0, 0)),
            pl.BlockSpec((1, 256, nk), lambda b: (b, 0, 0)),
            pl.BlockSpec((1, 24, nk), lambda b: (b, 0, 0)),
            pl.BlockSpec(memory_space=pl.ANY),
        ],
        out_specs=pl.BlockSpec((1, QT, 256), lambda b: (b, tile, 0)),
        out_shape=jax.ShapeDtypeStruct(hc.shape, hc.dtype),
        input_output_aliases={3: 0},
        scratch_shapes=[
            pltpu.VMEM((nk, QT), jnp.int32),
            pltpu.VMEM((nk, QT), F32),
            pltpu.VMEM((2, nk, QT), F32),
            pltpu.VMEM((nk, QT), BF16),
        ],
        compiler_params=_cparams(("parallel",)),
        name=f"dsa_{nk}",
    )(pc, vt, gt, hc)


def _dsa(pc, vt, gt):
    B, S, _ = pc.shape
    n_sel = min(TOPK_MAX, S // 4)
    hc = jnp.zeros((B, S, ATT_HEADS * ATT_DH), BF16)
    for nk in range(DSA_QT, S + 1, DSA_QT):
        hc = _dsa_tile(pc, vt, gt, hc, nk=nk, n_sel=n_sel)
    return hc


def _ffn_body(x_ref, ha_ref, hb_ref, hc_ref, woa_ref, wob_ref, woc_ref, gmix_ref,
              gpre_ref, wup_ref, cw_ref, cb_ref, wd_ref, gpost_ref, o_ref, h_ref, act_ref, halo_ref,
              *, tm, dff, tn, rb):
    i = pl.program_id(1)

    @pl.when(i == 0)
    def _():
        halo_ref[...] = jnp.zeros_like(halo_ref)

    def row_block(r, carry):
        r0 = pl.multiple_of(r * rb, rb)
        rows = pl.ds(r0, rb)
        mix = (_dot(ha_ref[0, rows, :], woa_ref[...]) + _dot(hb_ref[0, rows, :], wob_ref[...])
               + _dot(hc_ref[0, rows, :], woc_ref[...]))
        ms_mix = jnp.mean(mix * mix, axis=-1, keepdims=True)
        x1 = x_ref[0, rows, :] + mix * lax.rsqrt(ms_mix + EPS) * gmix_ref[...]
        o_ref[0, rows, :] = x1
        ms = jnp.mean(x1 * x1, axis=-1, keepdims=True)
        h_ref[...] = (x1 * lax.rsqrt(ms + EPS) * gpre_ref[...]).astype(BF16)

        def conv(col0, slot, c0):
            xu = _dot(h_ref[...], wup_ref[:, col0:col0 + tn])
            prev = halo_ref[slot, :, c0:c0 + tn]
            halo_ref[slot, :, c0:c0 + tn] = xu[rb - 8:rb]
            cat = jnp.concatenate([prev, xu], axis=0)
            y = xu * cw_ref[FFN_CONV - 1:FFN_CONV, col0:col0 + tn] + cb_ref[:, col0:col0 + tn]
            for d in range(1, FFN_CONV):
                y = y + pltpu.roll(cat, d, axis=0)[8:8 + rb] * cw_ref[FFN_CONV - 1 - d:FFN_CONV - d, col0:col0 + tn]
            return y

        for c0 in range(0, dff, tn):
            gate = conv(c0, 0, c0)
            up = conv(dff + c0, 1, c0)
            act_ref[:, c0:c0 + tn] = (_gelu_tanh(gate) * up).astype(BF16)
        y = _dot(act_ref[...], wd_ref[...])
        ms_y = jnp.mean(y * y, axis=-1, keepdims=True)
        o_ref[0, rows, :] = o_ref[0, rows, :] + y * lax.rsqrt(ms_y + EPS) * gpost_ref[...]
        return carry

    lax.fori_loop(0, tm // rb, row_block, 0)


def _mix_ffn(x, ha, hb, hc, wo_a, wo_b, wo_c, gmix, gpre, w_up, cw, cb, w_down, gpost, l, *, tm, tn, rb):
    B, S, D = x.shape
    dff = w_down.shape[1]
    row = lambda a: pl.BlockSpec((1, tm, a.shape[2]), lambda b, i: (b, i, 0))
    resident = lambda a: _layer_spec(a, l, pipeline_mode=pl.Buffered(1))
    weights = (wo_a, wo_b, wo_c, gmix, gpre, w_up, cw, cb, w_down, gpost)
    return pl.pallas_call(
        functools.partial(_ffn_body, tm=tm, dff=dff, tn=tn, rb=rb),
        grid=(B, S // tm),
        in_specs=[row(x), row(ha), row(hb), row(hc)] + [resident(w) for w in weights],
        out_specs=row(x),
        out_shape=jax.ShapeDtypeStruct((B, S, D), F32),
        scratch_shapes=[
            pltpu.VMEM((rb, D), BF16),
            pltpu.VMEM((rb, dff), BF16),
            pltpu.VMEM((2, 8, dff), F32),
        ],
        compiler_params=_cparams(("parallel", "arbitrary")),
        name="mix_ffn",
    )(x, ha, hb, hc, *weights)


def _pad_heads(w, n_heads, dh, pad):
    lead = w.shape[:-1]
    w = w.reshape(lead + (n_heads, dh))
    w = jnp.pad(w, [(0, 0)] * len(lead) + [(0, 0), (0, pad - dh)])
    return w.reshape(lead + (n_heads * pad,))


def _block_diag(w):
    depth, nb, bw, _ = w.shape
    eye = jnp.eye(nb, dtype=w.dtype)
    return jnp.einsum("lncd,nm->lncmd", w, eye).reshape(depth, nb * bw, nb * bw)


def _prepare(p):
    H, DH, P = MLSTM_HEADS, MLSTM_DH, HEAD_PAD
    depth, D, _ = p["w_in"].shape
    mw = H * DH
    lw = p["lru_lambda"].shape[-1]
    aw = ATT_HEADS * ATT_DH
    w_in = p["w_in"]
    offs = [0]
    for s in (mw, mw, mw, mw, H, H, lw, lw, aw, aw, aw, IDX_HEADS * IDX_DIM, IDX_DIM, IDX_HEADS):
        offs.append(offs[-1] + s)
    col = lambda k: w_in[:, :, offs[k]:offs[k + 1]]
    mq, mk, mv, mo, mi, mf, lx, lg, aq, ak, av, iq, ik, iw = [col(k) for k in range(14)]

    w_a = jnp.concatenate([_pad_heads(t, H, DH, P) for t in (mq, mk, mv, mo)], axis=2)
    w_l = jnp.concatenate([lx, lg], axis=2)
    w_c = jnp.concatenate([aq, ak, iq, ik, ik], axis=2)
    pad4 = lambda t: jnp.pad(t, ((0, 0), (0, 0), (0, 4)))
    w_t = jnp.concatenate([av, pad4(mi), pad4(mf), pad4(iw)], axis=2)
    row = lambda a: a.reshape(depth, 1, a.shape[-1])
    pad_col = lambda b: jnp.pad(b[:, :, None], ((0, 0), (0, 8 - H), (0, 0)))
    w_out = p["w_out"]
    wo_a = jnp.pad(w_out[:, 0:mw].reshape(depth, H, DH, D), ((0, 0), (0, 0), (0, P - DH), (0, 0)))
    return dict(
        widths=(w_a.shape[2], w_l.shape[2], w_c.shape[2]),
        w_main=jnp.concatenate([w_a, w_l, w_c], axis=2).astype(BF16),
        w_t=jnp.swapaxes(w_t, 1, 2).astype(BF16),
        g_mix_pre=row(p["norm_mix_pre"]), g_mix_post=row(p["norm_mix_post"]),
        g_ffn_pre=row(p["norm_ffn_pre"]), g_ffn_post=row(p["norm_ffn_post"]),
        b_i=pad_col(p["b_igate"]), b_f=pad_col(p["b_fgate"]),
        g_mlstm=_pad_heads(row(p["mlstm_norm"]), H, DH, P),
        lru_cw=p["lru_conv_w"], lru_cb=row(p["lru_conv_b"]),
        lru_wa=_block_diag(p["lru_w_a"]).astype(BF16), lru_ba=row(p["lru_b_a"]),
        lru_wx=_block_diag(p["lru_w_x"]).astype(BF16), lru_bx=row(p["lru_b_x"]),
        lru_lam=row(p["lru_lambda"]),
        wo_a=wo_a.reshape(depth, H * P, D).astype(BF16),
        wo_b=w_out[:, mw:mw + lw].astype(BF16), wo_c=w_out[:, mw + lw:].astype(BF16),
        ffn_up=p["ffn_up"].astype(BF16), ffn_cw=p["ffn_conv_w"], ffn_cb=row(p["ffn_conv_b"]),
        ffn_down=p["ffn_down"].astype(BF16),
    )


def _project(x, w, l):
    na, nl, nc = w["widths"]
    return _proj(x, w["g_mix_pre"], w["w_main"], w["w_t"], l, na=na, nl=nl, nc=nc, tm=min(512, x.shape[1]))


def _layer_parts(x, w, l):
    pa, plx, pc, vt, gt = _project(x, w, l)
    ha, hb = _mixers(pa, gt, w["b_i"], w["b_f"], w["g_mlstm"], plx, w["lru_cw"], w["lru_cb"],
                     w["lru_wa"], w["lru_ba"], w["lru_wx"], w["lru_bx"], w["lru_lam"], l)
    hc = _dsa(pc, vt, gt)
    return ha, hb, hc


def _layer(x, w, l):
    ha, hb, hc = _layer_parts(x, w, l)
    return _mix_ffn(x, ha, hb, hc, w["wo_a"], w["wo_b"], w["wo_c"], w["g_mix_post"], w["g_ffn_pre"],
                    w["ffn_up"], w["ffn_cw"], w["ffn_cb"], w["ffn_down"], w["g_ffn_post"], l,
                    tm=min(1024, x.shape[1]), tn=256, rb=512)


def kernel(x, norm_mix_pre, norm_mix_post, norm_ffn_pre, norm_ffn_post, w_in, b_igate, b_fgate,
           mlstm_norm, lru_conv_w, lru_conv_b, lru_w_a, lru_b_a, lru_w_x, lru_b_x, lru_lambda,
           w_out, ffn_up, ffn_conv_w, ffn_conv_b, ffn_down):
    w = _prepare(dict(norm_mix_pre=norm_mix_pre, norm_mix_post=norm_mix_post, norm_ffn_pre=norm_ffn_pre,
                      norm_ffn_post=norm_ffn_post, w_in=w_in, b_igate=b_igate, b_fgate=b_fgate,
                      mlstm_norm=mlstm_norm, lru_conv_w=lru_conv_w, lru_conv_b=lru_conv_b,
                      lru_w_a=lru_w_a, lru_b_a=lru_b_a, lru_w_x=lru_w_x, lru_b_x=lru_b_x,
                      lru_lambda=lru_lambda, w_out=w_out, ffn_up=ffn_up, ffn_conv_w=ffn_conv_w,
                      ffn_conv_b=ffn_conv_b, ffn_down=ffn_down))
    for l in range(w_in.shape[0]):
        x = _layer(x, w, l)
    return x
```

```python
import functools

import jax
import jax.numpy as jnp
from jax import lax
from jax.experimental import pallas as pl
from jax.experimental.pallas import tpu as pltpu

F32 = jnp.float32
BF16 = jnp.bfloat16

LANES = 128
CHUNK = 64
MLSTM_HEADS = 4
MLSTM_DH = 96
HEAD_PAD = LANES
LRU_BLOCKS = 6
LRU_CONV = 4
LRU_TILE = 64
RG_C = 8.0
ATT_HEADS = 4
ATT_DH = 64
IDX_HEADS = 4
IDX_DIM = 64
TOPK_MAX = 256
FFN_CONV = 3
EPS = 1e-6
VMEM_LIMIT = 56 * 1024 * 1024

NEG_INF = float("-inf")
INT_MIN = -2147483648
KEY_NEG_INF = -2139095040


def _cparams(sem):
    return pltpu.CompilerParams(dimension_semantics=sem, vmem_limit_bytes=VMEM_LIMIT)


def _dot(a, b):
    return jnp.dot(a, b, preferred_element_type=F32)


def _dot_nt(a, b):
    return lax.dot_general(a, b, (((1,), (1,)), ((), ())), preferred_element_type=F32)


def _split_dot(x, ones_b):
    hi = x.astype(BF16)
    r1 = x - hi.astype(F32)
    mid = r1.astype(BF16)
    lo = (r1 - mid.astype(F32)).astype(BF16)
    return _dot(hi, ones_b) + _dot(mid, ones_b) + _dot(lo, ones_b)


def _log_sigmoid(x):
    return jnp.minimum(x, 0.0) - jnp.log1p(jnp.exp(-jnp.abs(x)))


def _sigmoid(x):
    return 0.5 * jnp.tanh(0.5 * x) + 0.5


def _one_minus_sq_exp(log_a, a):
    x = 2.0 * log_a
    series = -x * (1.0 + x * (1 / 2 + x * (1 / 6 + x * (1 / 24 + x * (1 / 120)))))
    return jnp.where(x > -0.1, series, 1.0 - a * a)


def _gelu_tanh(x):
    c = 0.7978845608028654
    return 0.5 * x * (1.0 + jnp.tanh(c * (x + 0.044715 * (x * x * x))))


def _proj_body(x_ref, g_ref, w_ref, wt_ref, pa_ref, pl_ref, pc_ref, vt_ref, gt_ref,
               *, na, nl, nc):
    x = x_ref[0]
    ms = jnp.mean(x * x, axis=-1, keepdims=True)
    h = (x * lax.rsqrt(ms + EPS) * g_ref[...]).astype(BF16)
    step = 512
    for ref, base, width in ((pa_ref, 0, na), (pl_ref, na, nl), (pc_ref, na + nl, nc)):
        for lo in range(0, width, step):
            hi = min(lo + step, width)
            ref[0, :, lo:hi] = _dot(h, w_ref[:, base + lo:base + hi]).astype(ref.dtype)
    t = _dot_nt(wt_ref[...], h)
    vt_ref[0] = t[0:256].astype(BF16)
    gt_ref[0] = t[256:280]


def _layer_spec(a, l, **kw):
    zeros = (0,) * (a.ndim - 1)
    return pl.BlockSpec((None,) + a.shape[1:], lambda *_: (l,) + zeros, **kw)


def _proj(x, gain, w, wt, l, *, na, nl, nc, tm):
    B, S, D = x.shape
    grid = (B, S // tm)
    return pl.pallas_call(
        functools.partial(_proj_body, na=na, nl=nl, nc=nc),
        grid=grid,
        in_specs=[
            pl.BlockSpec((1, tm, D), lambda b, i: (b, i, 0)),
            _layer_spec(gain, l), _layer_spec(w, l), _layer_spec(wt, l),
        ],
        out_specs=[
            pl.BlockSpec((1, tm, na), lambda b, i: (b, i, 0)),
            pl.BlockSpec((1, tm, nl), lambda b, i: (b, i, 0)),
            pl.BlockSpec((1, tm, nc), lambda b, i: (b, i, 0)),
            pl.BlockSpec((1, 256, tm), lambda b, i: (b, 0, i)),
            pl.BlockSpec((1, 24, tm), lambda b, i: (b, 0, i)),
        ],
        out_shape=[
            jax.ShapeDtypeStruct((B, S, na), BF16),
            jax.ShapeDtypeStruct((B, S, nl), F32),
            jax.ShapeDtypeStruct((B, S, nc), BF16),
            jax.ShapeDtypeStruct((B, 256, S), BF16),
            jax.ShapeDtypeStruct((B, 24, S), F32),
        ],
        compiler_params=_cparams(("parallel", "parallel")),
        name="in_proj",
    )(x, gain, w, wt)


def _mixers_body(pa_ref, gt_ref, bi_ref, bf_ref, gain_ref,
                 pl_ref, cw_ref, cb_ref, wa_ref, ba_ref, wx_ref, bx_ref, lam_ref,
                 out_ref, outb_ref, c_ref, row_ref, bc_ref, rm_ref, mm_ref, *, n_pairs):
    H, P, L = MLSTM_HEADS, HEAD_PAD, CHUNK
    scale = MLSTM_DH ** -0.5
    c_ref[...] = jnp.zeros_like(c_ref)
    lru_refs = (pl_ref, cw_ref, cb_ref, wa_ref, ba_ref, wx_ref, bx_ref)
    sp = _lru_softplus_neg(lam_ref[...])

    r128 = lax.broadcasted_iota(jnp.int32, (2 * L, 2 * L), 0)
    c128 = lax.broadcasted_iota(jnp.int32, (2 * L, 2 * L), 1)
    triu_bd = jnp.where((r128 // L == c128 // L) & (r128 <= c128), 1.0, 0.0).astype(BF16)
    r64 = lax.broadcasted_iota(jnp.int32, (L, L), 0)
    c64 = lax.broadcasted_iota(jnp.int32, (L, L), 1)
    tril = c64 <= r64
    lane_p = lax.broadcasted_iota(jnp.int32, (L, P), 1)
    is_den = lane_p == MLSTM_DH
    is_feat = lane_p < MLSTM_DH
    is_last = lax.broadcasted_iota(jnp.int32, (8, L), 1) == L - 1

    def gate_step(j, carry):
        m_col, h_lru = carry
        h_lru = _lru_tile(lru_refs, outb_ref, sp, j, h_lru)
        col0 = pl.multiple_of(j * (2 * L), 2 * L)
        gi = gt_ref[0, 0:8, pl.ds(col0, 2 * L)] + bi_ref[...]
        lf = _log_sigmoid(gt_ref[0, 8:16, pl.ds(col0, 2 * L)] + bf_ref[...])
        bcum = _split_dot(lf, triu_bd)
        row_ref[0, :, pl.ds(col0, 2 * L)] = gi
        row_ref[1, :, pl.ds(col0, 2 * L)] = bcum
        for half in range(2):
            c = 2 * j + half
            row0 = pl.multiple_of(c * L, L)
            lsl = slice(half * L, (half + 1) * L)
            gi_h, bcum_h, lf_h = gi[:, lsl], bcum[:, lsl], lf[:, lsl]
            mm_ref[c] = jnp.broadcast_to(m_col, (8, P))
            b_last = jnp.sum(jnp.where(is_last, bcum_h, 0.0), axis=1, keepdims=True)
            gmax = jnp.max(b_last - bcum_h + gi_h, axis=1, keepdims=True)
            m_col = jnp.maximum(b_last + m_col, gmax)
            for h in range(H):
                lrow = jnp.broadcast_to(lf_h[h:h + 1], (L, L))
                bc = jnp.sum(jnp.where(tril, lrow, 0.0), axis=1, keepdims=True)
                d_log = jnp.where(tril, bc - bcum_h[h:h + 1] + gi_h[h:h + 1], NEG_INF)
                bc_ref[h, pl.ds(row0, L), :] = jnp.broadcast_to(bc, (L, P))
                rm_ref[h, pl.ds(row0, L), :] = jnp.broadcast_to(jnp.max(d_log, axis=1, keepdims=True), (L, P))
        return m_col, h_lru

    lru_c = cw_ref.shape[1]
    m_end, h_lru = lax.fori_loop(0, n_pairs, gate_step,
                                 (jnp.zeros((8, 1), F32), jnp.zeros((1, lru_c), F32)))
    mm_ref[2 * n_pairs] = jnp.broadcast_to(m_end, (8, P))

    def pair_step(j, h_lru):
        h_lru = _lru_tile(lru_refs, outb_ref, sp, n_pairs + j, h_lru)
        col0 = pl.multiple_of(j * (2 * L), 2 * L)
        gi = row_ref[0, :, pl.ds(col0, 2 * L)]
        bcum = row_ref[1, :, pl.ds(col0, 2 * L)]
        units = {}
        for half in range(2):
            c = 2 * j + half
            row0 = pl.multiple_of(c * L, L)
            lsl = slice(half * L, (half + 1) * L)
            gi_h, bcum_h = gi[:, lsl], bcum[:, lsl]
            m_prev_all = mm_ref[c]
            m_new_all = mm_ref[c + 1]
            for h in range(H):
                irow = gi_h[h:h + 1]
                brow = bcum_h[h:h + 1]
                m_prev = m_prev_all[h:h + 1]
                m_new = m_new_all[h:h + 1]
                bc = bc_ref[h, pl.ds(row0, L), :]
                q = pa_ref[0, pl.ds(row0, L), h * P:(h + 1) * P]
                k = pa_ref[0, pl.ds(row0, L), (H + h) * P:(H + h + 1) * P]
                v = pa_ref[0, pl.ds(row0, L), (2 * H + h) * P:(2 * H + h + 1) * P]
                v_aug = jnp.where(is_den, 1.0, v.astype(F32)).astype(BF16)
                kT = k.astype(F32).T
                b_last = bc[L - 1:L, :]
                grow = b_last[:, 0:L] - brow + irow
                wg = jnp.exp(grow - m_new[:, 0:L]) * scale
                decay = jnp.exp(b_last + m_prev - m_new)
                upd = _dot((kT * wg).astype(BF16), v_aug)
                d_log = jnp.where(tril, bc[:, 0:L] - brow + irow, NEG_INF)
                inter = bc + m_prev
                m_t = jnp.maximum(inter, rm_ref[h, pl.ds(row0, L), :])
                s = _dot(q, kT.astype(BF16)) * (jnp.exp(d_log - m_t[:, 0:L]) * scale)
                units[half, h] = (row0, q, v_aug, decay, upd, inter, m_t, s)

        for h in range(H):
            hs = slice(h * P, (h + 1) * P)
            c_state = c_ref[h]
            for half in range(2):
                row0, q, v_aug, decay, upd, inter, m_t, s = units[half, h]
                num = jnp.exp(inter - m_t) * _dot(q, c_state.astype(BF16)) + _dot(s.astype(BF16), v_aug)
                c_state = decay * c_state + upd
                den = jnp.sum(jnp.where(is_den, num, 0.0), axis=1, keepdims=True)
                feat = jnp.where(is_feat, num, 0.0)
                ssq = jnp.sum(feat * feat, axis=1, keepdims=True)
                dd = jnp.maximum(jnp.abs(den), jnp.exp(-m_t))
                inv = 1.0 / dd
                rs = inv * lax.rsqrt(ssq * (inv * inv) * (1.0 / MLSTM_DH) + EPS)
                o = pa_ref[0, pl.ds(row0, L), (3 * H + h) * P:(3 * H + h + 1) * P]
                y = feat * rs * gain_ref[:, hs] * _sigmoid(o.astype(F32))
                out_ref[0, pl.ds(row0, L), hs] = y.astype(BF16)
            c_ref[h] = c_state
        return h_lru

    lax.fori_loop(0, n_pairs, pair_step, h_lru)


def _mixers(pa, gt, bi, bf, gain, plx, cw, cb, wa, ba, wx, bx, lam, l):
    B, S, _ = pa.shape
    H, P = MLSTM_HEADS, HEAD_PAD
    C = lam.shape[-1]
    n_pairs = S // (2 * CHUNK)
    assert S // LRU_TILE == 2 * n_pairs
    full = lambda a: _layer_spec(a, l)
    return pl.pallas_call(
        functools.partial(_mixers_body, n_pairs=n_pairs),
        grid=(B,),
        in_specs=[
            pl.BlockSpec((1, S, 4 * H * P), lambda b: (b, 0, 0)),
            pl.BlockSpec((1, 24, S), lambda b: (b, 0, 0)),
            full(bi), full(bf), full(gain),
            pl.BlockSpec((1, S, 2 * C), lambda b: (b, 0, 0)),
            full(cw), full(cb), full(wa), full(ba), full(wx), full(bx), full(lam),
        ],
        out_specs=[pl.BlockSpec((1, S, H * P), lambda b: (b, 0, 0)),
                   pl.BlockSpec((1, S, C), lambda b: (b, 0, 0))],
        out_shape=[jax.ShapeDtypeStruct((B, S, H * P), BF16),
                   jax.ShapeDtypeStruct((B, S, C), BF16)],
        scratch_shapes=[
            pltpu.VMEM((H, P, P), F32),
            pltpu.VMEM((2, 8, S), F32),
            pltpu.VMEM((H, S, P), F32),
            pltpu.VMEM((H, S, P), F32),
            pltpu.VMEM((2 * n_pairs + 1, 8, P), F32),
        ],
        compiler_params=_cparams(("parallel",)),
        name="mixers_ab",
    )(pa, gt, bi, bf, gain, plx, cw, cb, wa, ba, wx, bx, lam)


def _lru_softplus_neg(lam):
    neg_lam = -lam
    return jnp.maximum(neg_lam, 0.0) + jnp.log1p(jnp.exp(-jnp.abs(neg_lam)))


def _lru_tile(lru_refs, out_ref, sp, t, h_last):
    pl_ref, cw_ref, cb_ref, wa_ref, ba_ref, wx_ref, bx_ref = lru_refs
    tr = LRU_TILE
    C = cw_ref.shape[1]
    sub = lax.broadcasted_iota(jnp.int32, (tr, C), 0) & 7
    r0 = pl.multiple_of(t * tr, tr)
    x = pl_ref[0, pl.ds(r0, tr), 0:C]
    lg = pl_ref[0, pl.ds(r0, tr), C:2 * C]
    rp = pl.multiple_of(jnp.maximum(r0 - 8, 0), 8)
    xp = pl_ref[0, pl.ds(rp, 8), 0:C] * jnp.where(t > 0, 1.0, 0.0)
    cat = jnp.concatenate([xp, x], axis=0)
    xc = x * cw_ref[LRU_CONV - 1:LRU_CONV, :] + cb_ref[...]
    for d in range(1, LRU_CONV):
        xs = pltpu.roll(cat, d, axis=0)[8:8 + tr]
        xc = xc + xs * cw_ref[LRU_CONV - 1 - d:LRU_CONV - d, :]
    xcb = xc.astype(BF16)
    r = _sigmoid(_dot(xcb, wa_ref[...]) + ba_ref[...])
    i = _sigmoid(_dot(xcb, wx_ref[...]) + bx_ref[...])
    log_a = -RG_C * r * sp
    a = jnp.exp(log_a)
    u = jnp.sqrt(_one_minus_sq_exp(log_a, a)) * (i * xc)
    for d in (1, 2, 4):
        a_s = pltpu.roll(a, d, axis=0)
        u_s = pltpu.roll(u, d, axis=0)
        ok = sub >= d
        u = jnp.where(ok, u + a * u_s, u)
        a = jnp.where(ok, a * a_s, a)
    outs = []
    carry = h_last
    for g in range(tr // 8):
        hg = u[g * 8:(g + 1) * 8] + a[g * 8:(g + 1) * 8] * carry
        carry = hg[7:8]
        outs.append(hg)
    hfull = jnp.concatenate(outs, axis=0)
    out_ref[0, pl.ds(r0, tr), :] = (hfull * _gelu_tanh(lg)).astype(BF16)
    return carry


def _col_reduce(x, op):
    n, w = x.shape
    g = 64 if n % 64 == 0 else 8
    part = op(x.reshape(n // g, g, w), axis=0)
    return op(part, axis=0, keepdims=True)


def _head_pair_block(t):
    lane = lax.broadcasted_iota(jnp.int32, t.shape, 1)
    zero = jnp.zeros_like(t)
    return jnp.concatenate([jnp.where(lane < 64, t, zero), jnp.where(lane >= 64, t, zero)], axis=0)


DSA_SLAB = 256
DSA_QT = 256


def _dsa_body(pc_ref, vt_ref, gt_ref, hc_in_ref, out_ref, key_ref, l_ref, e_ref, *, nk, n_sel, q0):
    del hc_in_ref
    QT = DSA_QT
    idx_scale = (IDX_HEADS ** -0.5) * (IDX_DIM ** -0.5)
    att_scale = ATT_DH ** -0.5

    aq_t = pc_ref[0, q0:q0 + QT, 0:256]
    iq_t = pc_ref[0, q0:q0 + QT, 512:768]
    iw = gt_ref[0, 16:24, q0:q0 + QT] * idx_scale
    qlane = lax.broadcasted_iota(jnp.int32, (1, QT), 1)
    lim = ((q0 + qlane) // CHUNK + 1) * CHUNK
    slabs = [(r0, min(DSA_SLAB, nk - r0)) for r0 in range(0, nk, DSA_SLAB)]

    iblk = [_head_pair_block(iq_t[:, hp * 128:(hp + 1) * 128]) for hp in range(IDX_HEADS // 2)]
    for r0, n in slabs:
        ik2 = pc_ref[0, r0:r0 + n, 768:896]
        score = jnp.zeros((n, QT), F32)
        for hp in range(IDX_HEADS // 2):
            r = _dot_nt(ik2, iblk[hp])
            score = score + jnp.maximum(r[:, 0:QT], 0.0) * iw[2 * hp:2 * hp + 1, :]
            score = score + jnp.maximum(r[:, QT:2 * QT], 0.0) * iw[2 * hp + 1:2 * hp + 2, :]
        if r0 + n > q0 + CHUNK:
            krow = r0 + lax.broadcasted_iota(jnp.int32, (n, QT), 0)
            score = jnp.where(krow < lim, score, NEG_INF)
        bits = pltpu.bitcast(score, jnp.int32)
        key_ref[r0:r0 + n, :] = jnp.where(bits >= 0, bits, jnp.int32(INT_MIN) - bits)

    def bit_step(i, carry):
        t, c_t = carry
        cand = t + lax.shift_left(jnp.int32(1), 31 - i)
        acc = jnp.zeros((64, QT), F32)
        for r0 in range(0, nk, 64):
            acc = acc + jnp.where(key_ref[r0:r0 + 64, :] >= cand, 1.0, 0.0)
        cnt = jnp.sum(acc, axis=0, keepdims=True)
        ok = cnt >= n_sel
        return jnp.where(ok, cand, t), jnp.where(ok, cnt, c_t)

    tau, cnt_ge = jnp.full((1, QT), INT_MIN, jnp.int32), jnp.full((1, QT), float(nk), F32)
    if nk > n_sel:
        tau, cnt_ge = lax.fori_loop(0, 32, bit_step, (tau, cnt_ge))
    valid = tau > KEY_NEG_INF
    tau_t = jnp.maximum(tau, KEY_NEG_INF)
    drop = jnp.where(valid, cnt_ge - n_sel, float(nk))
    tri_r = lax.broadcasted_iota(jnp.int32, (DSA_SLAB, DSA_SLAB), 0)
    tri_c = lax.broadcasted_iota(jnp.int32, (DSA_SLAB, DSA_SLAB), 1)
    tri = jnp.where(tri_c >= tri_r, 1.0, 0.0).astype(BF16)
    ablk = [_head_pair_block(aq_t[:, hp * 128:(hp + 1) * 128] * jnp.asarray(att_scale, BF16))
            for hp in range(ATT_HEADS // 2)]
    mx = [jnp.full((64, QT), NEG_INF, F32) for _ in range(ATT_HEADS)]
    later = jnp.zeros((1, QT), F32)
    for r0, n in reversed(slabs):
        key = key_ref[r0:r0 + n, :]
        eq = jnp.where(key == tau_t, 1.0, 0.0)
        rank_slab = _dot(tri[0:n, 0:n], eq.astype(BF16))
        tie_bias = jnp.where(rank_slab > drop - later, 0.0, NEG_INF)
        later = later + rank_slab[0:1, :]
        b = jnp.where(key > tau_t, 0.0, jnp.where(key == tau_t, tie_bias, NEG_INF))
        for hp in range(ATT_HEADS // 2):
            lg = _dot_nt(pc_ref[0, r0:r0 + n, 256 + hp * 128:256 + (hp + 1) * 128], ablk[hp])
            for sub in range(2):
                h = 2 * hp + sub
                l = lg[:, sub * QT:(sub + 1) * QT] + b
                l_ref[h, r0:r0 + n, :] = l
                mx[h] = jnp.maximum(mx[h], jnp.max(l.reshape(n // 64, 64, QT), axis=0))

    outs = []
    for h in range(ATT_HEADS):
        m = jnp.max(mx[h], axis=0, keepdims=True)
        den = jnp.zeros((64, QT), F32)
        for r0, n in slabs:
            e = jnp.exp(l_ref[h, r0:r0 + n, :] - m)
            den = den + jnp.sum(e.reshape(n // 64, 64, QT), axis=0)
            e_ref[r0:r0 + n, :] = e.astype(BF16)
        o_t = _dot(vt_ref[0, h * ATT_DH:(h + 1) * ATT_DH, 0:nk], e_ref[...])
        outs.append(o_t / jnp.sum(den, axis=0, keepdims=True))
    out_t = jnp.concatenate(outs, axis=0)
    out_ref[0] = out_t.T.astype(BF16)


def _dsa_tile(pc, vt, gt, hc, *, nk, n_sel):
    B, S, ncols = pc.shape
    QT = DSA_QT
    tile = nk // QT - 1
    return pl.pallas_call(
        functools.partial(_dsa_body, nk=nk, n_sel=n_sel, q0=nk - QT),
        grid=(B,),
        in_specs=[
            pl.BlockSpec((1, nk, ncols), lambda b: (b, 0, 0)),
            pl.BlockSpec((1, 256, nk), lambda b: (b, 0, 0)),
            pl.BlockSpec((1, 24, nk), lambda b: (b, 0, 0)),
            pl.BlockSpec(memory_space=pl.ANY),
        ],
        out_specs=pl.BlockSpec((1, QT, 256), lambda b: (b, tile, 0)),
        out_shape=jax.ShapeDtypeStruct(hc.shape, hc.dtype),
        input_output_aliases={3: 0},
        scratch_shapes=[
            pltpu.VMEM((nk, QT), jnp.int32),
            pltpu.VMEM((ATT_HEADS, nk, QT), F32),
            pltpu.VMEM((nk, QT), BF16),
        ],
        compiler_params=_cparams(("parallel",)),
        name=f"dsa_{nk}",
    )(pc, vt, gt, hc)


def _dsa(pc, vt, gt):
    B, S, _ = pc.shape
    n_sel = min(TOPK_MAX, S // 4)
    hc = jnp.zeros((B, S, ATT_HEADS * ATT_DH), BF16)
    for nk in range(DSA_QT, S + 1, DSA_QT):
        hc = _dsa_tile(pc, vt, gt, hc, nk=nk, n_sel=n_sel)
    return hc


def _ffn_body(x_ref, ha_ref, hb_ref, hc_ref, woa_ref, wob_ref, woc_ref, gmix_ref,
              gpre_ref, wup_ref, cw_ref, cb_ref, wd_ref, gpost_ref, o_ref, h_ref, act_ref, halo_ref,
              *, tm, dff, tn, rb):
    i = pl.program_id(1)

    @pl.when(i == 0)
    def _():
        halo_ref[...] = jnp.zeros_like(halo_ref)

    def row_block(r, carry):
        r0 = pl.multiple_of(r * rb, rb)
        rows = pl.ds(r0, rb)
        mix = (_dot(ha_ref[0, rows, :], woa_ref[...]) + _dot(hb_ref[0, rows, :], wob_ref[...])
               + _dot(hc_ref[0, rows, :], woc_ref[...]))
        ms_mix = jnp.mean(mix * mix, axis=-1, keepdims=True)
        x1 = x_ref[0, rows, :] + mix * lax.rsqrt(ms_mix + EPS) * gmix_ref[...]
        o_ref[0, rows, :] = x1
        ms = jnp.mean(x1 * x1, axis=-1, keepdims=True)
        h_ref[...] = (x1 * lax.rsqrt(ms + EPS) * gpre_ref[...]).astype(BF16)

        def conv(col0, slot, c0):
            xu = _dot(h_ref[...], wup_ref[:, col0:col0 + tn])
            prev = halo_ref[slot, :, c0:c0 + tn]
            halo_ref[slot, :, c0:c0 + tn] = xu[rb - 8:rb]
            cat = jnp.concatenate([prev, xu], axis=0)
            y = xu * cw_ref[FFN_CONV - 1:FFN_CONV, col0:col0 + tn] + cb_ref[:, col0:col0 + tn]
            for d in range(1, FFN_CONV):
                y = y + pltpu.roll(cat, d, axis=0)[8:8 + rb] * cw_ref[FFN_CONV - 1 - d:FFN_CONV - d, col0:col0 + tn]
            return y

        for c0 in range(0, dff, tn):
            gate = conv(c0, 0, c0)
            up = conv(dff + c0, 1, c0)
            act_ref[:, c0:c0 + tn] = (_gelu_tanh(gate) * up).astype(BF16)
        y = _dot(act_ref[...], wd_ref[...])
        ms_y = jnp.mean(y * y, axis=-1, keepdims=True)
        o_ref[0, rows, :] = o_ref[0, rows, :] + y * lax.rsqrt(ms_y + EPS) * gpost_ref[...]
        return carry

    lax.fori_loop(0, tm // rb, row_block, 0)


def _mix_ffn(x, ha, hb, hc, wo_a, wo_b, wo_c, gmix, gpre, w_up, cw, cb, w_down, gpost, l, *, tm, tn, rb):
    B, S, D = x.shape
    dff = w_down.shape[1]
    row = lambda a: pl.BlockSpec((1, tm, a.shape[2]), lambda b, i: (b, i, 0))
    resident = lambda a: _layer_spec(a, l, pipeline_mode=pl.Buffered(1))
    weights = (wo_a, wo_b, wo_c, gmix, gpre, w_up, cw, cb, w_down, gpost)
    return pl.pallas_call(
        functools.partial(_ffn_body, tm=tm, dff=dff, tn=tn, rb=rb),
        grid=(B, S // tm),
        in_specs=[row(x), row(ha), row(hb), row(hc)] + [resident(w) for w in weights],
        out_specs=row(x),
        out_shape=jax.ShapeDtypeStruct((B, S, D), F32),
        scratch_shapes=[
            pltpu.VMEM((rb, D), BF16),
            pltpu.VMEM((rb, dff), BF16),
            pltpu.VMEM((2, 8, dff), F32),
        ],
        compiler_params=_cparams(("parallel", "arbitrary")),
        name="mix_ffn",
    )(x, ha, hb, hc, *weights)


def _pad_heads(w, n_heads, dh, pad):
    lead = w.shape[:-1]
    w = w.reshape(lead + (n_heads, dh))
    w = jnp.pad(w, [(0, 0)] * len(lead) + [(0, 0), (0, pad - dh)])
    return w.reshape(lead + (n_heads * pad,))


def _block_diag(w):
    depth, nb, bw, _ = w.shape
    eye = jnp.eye(nb, dtype=w.dtype)
    return jnp.einsum("lncd,nm->lncmd", w, eye).reshape(depth, nb * bw, nb * bw)


def _prepare(p):
    H, DH, P = MLSTM_HEADS, MLSTM_DH, HEAD_PAD
    depth, D, _ = p["w_in"].shape
    mw = H * DH
    lw = p["lru_lambda"].shape[-1]
    aw = ATT_HEADS * ATT_DH
    w_in = p["w_in"]
    offs = [0]
    for s in (mw, mw, mw, mw, H, H, lw, lw, aw, aw, aw, IDX_HEADS * IDX_DIM, IDX_DIM, IDX_HEADS):
        offs.append(offs[-1] + s)
    col = lambda k: w_in[:, :, offs[k]:offs[k + 1]]
    mq, mk, mv, mo, mi, mf, lx, lg, aq, ak, av, iq, ik, iw = [col(k) for k in range(14)]

    w_a = jnp.concatenate([_pad_heads(t, H, DH, P) for t in (mq, mk, mv, mo)], axis=2)
    w_l = jnp.concatenate([lx, lg], axis=2)
    w_c = jnp.concatenate([aq, ak, iq, ik, ik], axis=2)
    pad4 = lambda t: jnp.pad(t, ((0, 0), (0, 0), (0, 4)))
    w_t = jnp.concatenate([av, pad4(mi), pad4(mf), pad4(iw)], axis=2)
    row = lambda a: a.reshape(depth, 1, a.shape[-1])
    pad_col = lambda b: jnp.pad(b[:, :, None], ((0, 0), (0, 8 - H), (0, 0)))
    w_out = p["w_out"]
    wo_a = jnp.pad(w_out[:, 0:mw].reshape(depth, H, DH, D), ((0, 0), (0, 0), (0, P - DH), (0, 0)))
    return dict(
        widths=(w_a.shape[2], w_l.shape[2], w_c.shape[2]),
        w_main=jnp.concatenate([w_a, w_l, w_c], axis=2).astype(BF16),
        w_t=jnp.swapaxes(w_t, 1, 2).astype(BF16),
        g_mix_pre=row(p["norm_mix_pre"]), g_mix_post=row(p["norm_mix_post"]),
        g_ffn_pre=row(p["norm_ffn_pre"]), g_ffn_post=row(p["norm_ffn_post"]),
        b_i=pad_col(p["b_igate"]), b_f=pad_col(p["b_fgate"]),
        g_mlstm=_pad_heads(row(p["mlstm_norm"]), H, DH, P),
        lru_cw=p["lru_conv_w"], lru_cb=row(p["lru_conv_b"]),
        lru_wa=_block_diag(p["lru_w_a"]).astype(BF16), lru_ba=row(p["lru_b_a"]),
        lru_wx=_block_diag(p["lru_w_x"]).astype(BF16), lru_bx=row(p["lru_b_x"]),
        lru_lam=row(p["lru_lambda"]),
        wo_a=wo_a.reshape(depth, H * P, D).astype(BF16),
        wo_b=w_out[:, mw:mw + lw].astype(BF16), wo_c=w_out[:, mw + lw:].astype(BF16),
        ffn_up=p["ffn_up"].astype(BF16), ffn_cw=p["ffn_conv_w"], ffn_cb=row(p["ffn_conv_b"]),
        ffn_down=p["ffn_down"].astype(BF16),
    )


def _project(x, w, l):
    na, nl, nc = w["widths"]
    return _proj(x, w["g_mix_pre"], w["w_main"], w["w_t"], l, na=na, nl=nl, nc=nc, tm=min(512, x.shape[1]))


def _layer_parts(x, w, l):
    pa, plx, pc, vt, gt = _project(x, w, l)
    ha, hb = _mixers(pa, gt, w["b_i"], w["b_f"], w["g_mlstm"], plx, w["lru_cw"], w["lru_cb"],
                     w["lru_wa"], w["lru_ba"], w["lru_wx"], w["lru_bx"], w["lru_lam"], l)
    hc = _dsa(pc, vt, gt)
    return ha, hb, hc


def _layer(x, w, l):
    ha, hb, hc = _layer_parts(x, w, l)
    return _mix_ffn(x, ha, hb, hc, w["wo_a"], w["wo_b"], w["wo_c"], w["g_mix_post"], w["g_ffn_pre"],
                    w["ffn_up"], w["ffn_cw"], w["ffn_cb"], w["ffn_down"], w["g_ffn_post"], l,
                    tm=min(1024, x.shape[1]), tn=256, rb=512)


def kernel(x, norm_mix_pre, norm_mix_post, norm_ffn_pre, norm_ffn_post, w_in, b_igate, b_fgate,
           mlstm_norm, lru_conv_w, lru_conv_b, lru_w_a, lru_b_a, lru_w_x, lru_b_x, lru_lambda,
           w_out, ffn_up, ffn_conv_w, ffn_conv_b, ffn_down):
    w = _prepare(dict(norm_mix_pre=norm_mix_pre, norm_mix_post=norm_mix_post, norm_ffn_pre=norm_ffn_pre,
                      norm_ffn_post=norm_ffn_post, w_in=w_in, b_igate=b_igate, b_fgate=b_fgate,
                      mlstm_norm=mlstm_norm, lru_conv_w=lru_conv_w, lru_conv_b=lru_conv_b,
                      lru_w_a=lru_w_a, lru_b_a=lru_b_a, lru_w_x=lru_w_x, lru_b_x=lru_b_x,
                      lru_lambda=lru_lambda, w_out=w_out, ffn_up=ffn_up, ffn_conv_w=ffn_conv_w,
                      ffn_conv_b=ffn_conv_b, ffn_down=ffn_down))
    for l in range(w_in.shape[0]):
        x = _layer(x, w, l)
    return x
```

```python
import functools

import jax
import jax.numpy as jnp
from jax import lax
from jax.experimental import pallas as pl
from jax.experimental.pallas import tpu as pltpu

F32 = jnp.float32
BF16 = jnp.bfloat16

LANES = 128
CHUNK = 64
MLSTM_HEADS = 4
MLSTM_DH = 96
HEAD_PAD = LANES
LRU_CONV = 4
LRU_TILE = 64
RG_C = 8.0
ATT_HEADS = 4
ATT_DH = 64
IDX_HEADS = 4
IDX_DIM = 64
TOPK_MAX = 256
FFN_CONV = 3
EPS = 1e-6
VMEM_LIMIT = 56 * 1024 * 1024

NEG_INF = float("-inf")
INT_MIN = -2147483648
KEY_NEG_INF = -2139095040


def _cparams(sem):
    return pltpu.CompilerParams(dimension_semantics=sem, vmem_limit_bytes=VMEM_LIMIT)


def _dot(a, b):
    return jnp.dot(a, b, preferred_element_type=F32)


def _dot_nt(a, b):
    return lax.dot_general(a, b, (((1,), (1,)), ((), ())), preferred_element_type=F32)


def _split_dot(x, ones_b):
    hi = x.astype(BF16)
    r1 = x - hi.astype(F32)
    mid = r1.astype(BF16)
    lo = (r1 - mid.astype(F32)).astype(BF16)
    return _dot(hi, ones_b) + _dot(mid, ones_b) + _dot(lo, ones_b)


def _log_sigmoid(x):
    return jnp.minimum(x, 0.0) - jnp.log1p(jnp.exp(-jnp.abs(x)))


def _sigmoid(x):
    return 0.5 * jnp.tanh(0.5 * x) + 0.5


def _one_minus_sq_exp(log_a, a):
    x = 2.0 * log_a
    series = -x * (1.0 + x * (1 / 2 + x * (1 / 6 + x * (1 / 24 + x * (1 / 120)))))
    return jnp.where(x > -0.1, series, 1.0 - a * a)


def _gelu_tanh(x):
    c = 0.7978845608028654
    return 0.5 * x * (1.0 + jnp.tanh(c * (x + 0.044715 * (x * x * x))))


def _proj_body(x_ref, g_ref, w_ref, wt_ref, pa_ref, pl_ref, pc_ref, vt_ref, gt_ref,
               *, na, nl, nc):
    x = x_ref[0]
    ms = jnp.mean(x * x, axis=-1, keepdims=True)
    h = (x * lax.rsqrt(ms + EPS) * g_ref[...]).astype(BF16)
    step = 512
    for ref, base, width in ((pa_ref, 0, na), (pl_ref, na, nl), (pc_ref, na + nl, nc)):
        for lo in range(0, width, step):
            hi = min(lo + step, width)
            ref[0, :, lo:hi] = _dot(h, w_ref[:, base + lo:base + hi]).astype(ref.dtype)
    t = _dot_nt(wt_ref[...], h)
    vt_ref[0] = t[0:256].astype(BF16)
    gt_ref[0] = t[256:280]


def _layer_spec(a, l, **kw):
    zeros = (0,) * (a.ndim - 1)
    return pl.BlockSpec((None,) + a.shape[1:], lambda *_: (l,) + zeros, **kw)


def _proj(x, gain, w, wt, l, *, na, nl, nc, tm):
    B, S, D = x.shape
    grid = (B, S // tm)
    return pl.pallas_call(
        functools.partial(_proj_body, na=na, nl=nl, nc=nc),
        grid=grid,
        in_specs=[
            pl.BlockSpec((1, tm, D), lambda b, i: (b, i, 0)),
            _layer_spec(gain, l), _layer_spec(w, l), _layer_spec(wt, l),
        ],
        out_specs=[
            pl.BlockSpec((1, tm, na), lambda b, i: (b, i, 0)),
            pl.BlockSpec((1, tm, nl), lambda b, i: (b, i, 0)),
            pl.BlockSpec((1, tm, nc), lambda b, i: (b, i, 0)),
            pl.BlockSpec((1, 256, tm), lambda b, i: (b, 0, i)),
            pl.BlockSpec((1, 24, tm), lambda b, i: (b, 0, i)),
        ],
        out_shape=[
            jax.ShapeDtypeStruct((B, S, na), BF16),
            jax.ShapeDtypeStruct((B, S, nl), F32),
            jax.ShapeDtypeStruct((B, S, nc), BF16),
            jax.ShapeDtypeStruct((B, 256, S), BF16),
            jax.ShapeDtypeStruct((B, 24, S), F32),
        ],
        compiler_params=_cparams(("parallel", "parallel")),
        name="in_proj",
    )(x, gain, w, wt)


def _mixers_body(pa_ref, gt_ref, bi_ref, bf_ref, gain_ref,
                 pl_ref, cw_ref, cb_ref, wa_ref, ba_ref, wx_ref, bx_ref, lam_ref,
                 out_ref, outb_ref, c_ref, row_ref, bc_ref, rm_ref, mm_ref, *, n_pairs):
    H, P, L = MLSTM_HEADS, HEAD_PAD, CHUNK
    scale = MLSTM_DH ** -0.5
    c_ref[...] = jnp.zeros_like(c_ref)
    lru_refs = (pl_ref, cw_ref, cb_ref, wa_ref, ba_ref, wx_ref, bx_ref)
    sp = _lru_softplus_neg(lam_ref[...])

    r128 = lax.broadcasted_iota(jnp.int32, (2 * L, 2 * L), 0)
    c128 = lax.broadcasted_iota(jnp.int32, (2 * L, 2 * L), 1)
    triu_bd = jnp.where((r128 // L == c128 // L) & (r128 <= c128), 1.0, 0.0).astype(BF16)
    r64 = lax.broadcasted_iota(jnp.int32, (L, L), 0)
    c64 = lax.broadcasted_iota(jnp.int32, (L, L), 1)
    tril = c64 <= r64
    lane_p = lax.broadcasted_iota(jnp.int32, (L, P), 1)
    is_den = lane_p == MLSTM_DH
    is_feat = lane_p < MLSTM_DH
    is_last = lax.broadcasted_iota(jnp.int32, (8, L), 1) == L - 1

    def gate_step(j, carry):
        m_col, h_lru = carry
        h_lru = _lru_tile(lru_refs, outb_ref, sp, j, h_lru)
        col0 = pl.multiple_of(j * (2 * L), 2 * L)
        gi = gt_ref[0, 0:8, pl.ds(col0, 2 * L)] + bi_ref[...]
        lf = _log_sigmoid(gt_ref[0, 8:16, pl.ds(col0, 2 * L)] + bf_ref[...])
        bcum = _split_dot(lf, triu_bd)
        row_ref[0, :, pl.ds(col0, 2 * L)] = gi
        row_ref[1, :, pl.ds(col0, 2 * L)] = bcum
        for half in range(2):
            c = 2 * j + half
            row0 = pl.multiple_of(c * L, L)
            lsl = slice(half * L, (half + 1) * L)
            gi_h, bcum_h, lf_h = gi[:, lsl], bcum[:, lsl], lf[:, lsl]
            mm_ref[c] = jnp.broadcast_to(m_col, (8, P))
            b_last = jnp.sum(jnp.where(is_last, bcum_h, 0.0), axis=1, keepdims=True)
            gmax = jnp.max(b_last - bcum_h + gi_h, axis=1, keepdims=True)
            m_col = jnp.maximum(b_last + m_col, gmax)
            for h in range(H):
                lrow = jnp.broadcast_to(lf_h[h:h + 1], (L, L))
                bc = jnp.sum(jnp.where(tril, lrow, 0.0), axis=1, keepdims=True)
                d_log = jnp.where(tril, bc - bcum_h[h:h + 1] + gi_h[h:h + 1], NEG_INF)
                bc_ref[h, pl.ds(row0, L), :] = jnp.broadcast_to(bc, (L, P))
                rm_ref[h, pl.ds(row0, L), :] = jnp.broadcast_to(jnp.max(d_log, axis=1, keepdims=True), (L, P))
        return m_col, h_lru

    lru_c = cw_ref.shape[1]
    m_end, h_lru = lax.fori_loop(0, n_pairs, gate_step,
                                 (jnp.zeros((8, 1), F32), jnp.zeros((1, lru_c), F32)))
    mm_ref[2 * n_pairs] = jnp.broadcast_to(m_end, (8, P))

    def pair_step(j, h_lru):
        h_lru = _lru_tile(lru_refs, outb_ref, sp, n_pairs + j, h_lru)
        col0 = pl.multiple_of(j * (2 * L), 2 * L)
        gi = row_ref[0, :, pl.ds(col0, 2 * L)]
        bcum = row_ref[1, :, pl.ds(col0, 2 * L)]
        units = {}
        for half in range(2):
            c = 2 * j + half
            row0 = pl.multiple_of(c * L, L)
            lsl = slice(half * L, (half + 1) * L)
            gi_h, bcum_h = gi[:, lsl], bcum[:, lsl]
            m_prev_all = mm_ref[c]
            m_new_all = mm_ref[c + 1]
            for h in range(H):
                irow = gi_h[h:h + 1]
                brow = bcum_h[h:h + 1]
                m_prev = m_prev_all[h:h + 1]
                m_new = m_new_all[h:h + 1]
                bc = bc_ref[h, pl.ds(row0, L), :]
                q = pa_ref[0, pl.ds(row0, L), h * P:(h + 1) * P]
                k = pa_ref[0, pl.ds(row0, L), (H + h) * P:(H + h + 1) * P]
                v = pa_ref[0, pl.ds(row0, L), (2 * H + h) * P:(2 * H + h + 1) * P]
                v_aug = jnp.where(is_den, 1.0, v.astype(F32)).astype(BF16)
                kT = k.astype(F32).T
                b_last = bc[L - 1:L, :]
                grow = b_last[:, 0:L] - brow + irow
                wg = jnp.exp(grow - m_new[:, 0:L]) * scale
                decay = jnp.exp(b_last + m_prev - m_new)
                upd = _dot((kT * wg).astype(BF16), v_aug)
                d_log = jnp.where(tril, bc[:, 0:L] - brow + irow, NEG_INF)
                inter = bc + m_prev
                m_t = jnp.maximum(inter, rm_ref[h, pl.ds(row0, L), :])
                s = _dot(q, kT.astype(BF16)) * (jnp.exp(d_log - m_t[:, 0:L]) * scale)
                units[half, h] = (row0, q, v_aug, decay, upd, inter, m_t, s)

        for h in range(H):
            hs = slice(h * P, (h + 1) * P)
            c_state = c_ref[h]
            for half in range(2):
                row0, q, v_aug, decay, upd, inter, m_t, s = units[half, h]
                num = jnp.exp(inter - m_t) * _dot(q, c_state.astype(BF16)) + _dot(s.astype(BF16), v_aug)
                c_state = decay * c_state + upd
                den = jnp.sum(jnp.where(is_den, num, 0.0), axis=1, keepdims=True)
                feat = jnp.where(is_feat, num, 0.0)
                ssq = jnp.sum(feat * feat, axis=1, keepdims=True)
                dd = jnp.maximum(jnp.abs(den), jnp.exp(-m_t))
                inv = 1.0 / dd
                rs = inv * lax.rsqrt(ssq * (inv * inv) * (1.0 / MLSTM_DH) + EPS)
                o = pa_ref[0, pl.ds(row0, L), (3 * H + h) * P:(3 * H + h + 1) * P]
                y = feat * rs * gain_ref[:, hs] * _sigmoid(o.astype(F32))
                out_ref[0, pl.ds(row0, L), hs] = y.astype(BF16)
            c_ref[h] = c_state
        return h_lru

    lax.fori_loop(0, n_pairs, pair_step, h_lru)


def _mixers(pa, gt, bi, bf, gain, plx, cw, cb, wa, ba, wx, bx, lam, l):
    B, S, _ = pa.shape
    H, P = MLSTM_HEADS, HEAD_PAD
    C = lam.shape[-1]
    n_pairs = S // (2 * CHUNK)
    assert S // LRU_TILE == 2 * n_pairs
    full = lambda a: _layer_spec(a, l)
    return pl.pallas_call(
        functools.partial(_mixers_body, n_pairs=n_pairs),
        grid=(B,),
        in_specs=[
            pl.BlockSpec((1, S, 4 * H * P), lambda b: (b, 0, 0)),
            pl.BlockSpec((1, 24, S), lambda b: (b, 0, 0)),
            full(bi), full(bf), full(gain),
            pl.BlockSpec((1, S, 2 * C), lambda b: (b, 0, 0)),
            full(cw), full(cb), full(wa), full(ba), full(wx), full(bx), full(lam),
        ],
        out_specs=[pl.BlockSpec((1, S, H * P), lambda b: (b, 0, 0)),
                   pl.BlockSpec((1, S, C), lambda b: (b, 0, 0))],
        out_shape=[jax.ShapeDtypeStruct((B, S, H * P), BF16),
                   jax.ShapeDtypeStruct((B, S, C), BF16)],
        scratch_shapes=[
            pltpu.VMEM((H, P, P), F32),
            pltpu.VMEM((2, 8, S), F32),
            pltpu.VMEM((H, S, P), F32),
            pltpu.VMEM((H, S, P), F32),
            pltpu.VMEM((2 * n_pairs + 1, 8, P), F32),
        ],
        compiler_params=_cparams(("parallel",)),
        name="mixers_ab",
    )(pa, gt, bi, bf, gain, plx, cw, cb, wa, ba, wx, bx, lam)


def _lru_softplus_neg(lam):
    neg_lam = -lam
    return jnp.maximum(neg_lam, 0.0) + jnp.log1p(jnp.exp(-jnp.abs(neg_lam)))


def _lru_tile(lru_refs, out_ref, sp, t, h_last):
    pl_ref, cw_ref, cb_ref, wa_ref, ba_ref, wx_ref, bx_ref = lru_refs
    tr = LRU_TILE
    C = cw_ref.shape[1]
    sub = lax.broadcasted_iota(jnp.int32, (tr, C), 0) & 7
    r0 = pl.multiple_of(t * tr, tr)
    x = pl_ref[0, pl.ds(r0, tr), 0:C]
    lg = pl_ref[0, pl.ds(r0, tr), C:2 * C]
    rp = pl.multiple_of(jnp.maximum(r0 - 8, 0), 8)
    xp = pl_ref[0, pl.ds(rp, 8), 0:C] * jnp.where(t > 0, 1.0, 0.0)
    cat = jnp.concatenate([xp, x], axis=0)
    xc = x * cw_ref[LRU_CONV - 1:LRU_CONV, :] + cb_ref[...]
    for d in range(1, LRU_CONV):
        xs = pltpu.roll(cat, d, axis=0)[8:8 + tr]
        xc = xc + xs * cw_ref[LRU_CONV - 1 - d:LRU_CONV - d, :]
    xcb = xc.astype(BF16)
    r = _sigmoid(_dot(xcb, wa_ref[...]) + ba_ref[...])
    i = _sigmoid(_dot(xcb, wx_ref[...]) + bx_ref[...])
    log_a = -RG_C * r * sp
    a = jnp.exp(log_a)
    u = jnp.sqrt(_one_minus_sq_exp(log_a, a)) * (i * xc)
    for d in (1, 2, 4):
        a_s = pltpu.roll(a, d, axis=0)
        u_s = pltpu.roll(u, d, axis=0)
        ok = sub >= d
        u = jnp.where(ok, u + a * u_s, u)
        a = jnp.where(ok, a * a_s, a)
    outs = []
    carry = h_last
    for g in range(tr // 8):
        hg = u[g * 8:(g + 1) * 8] + a[g * 8:(g + 1) * 8] * carry
        carry = hg[7:8]
        outs.append(hg)
    hfull = jnp.concatenate(outs, axis=0)
    out_ref[0, pl.ds(r0, tr), :] = (hfull * _gelu_tanh(lg)).astype(BF16)
    return carry


def _head_pair_block(t):
    lane = lax.broadcasted_iota(jnp.int32, t.shape, 1)
    zero = jnp.zeros_like(t)
    return jnp.concatenate([jnp.where(lane < 64, t, zero), jnp.where(lane >= 64, t, zero)], axis=0)


DSA_SLAB = 256
DSA_QT = 256


def _dsa_body(pc_ref, vt_ref, gt_ref, hc_in_ref, out_ref, key_ref, l_ref, e_ref, *, nk, n_sel, q0):
    del hc_in_ref
    QT = DSA_QT
    idx_scale = (IDX_HEADS ** -0.5) * (IDX_DIM ** -0.5)
    att_scale = ATT_DH ** -0.5

    aq_t = pc_ref[0, q0:q0 + QT, 0:256]
    iq_t = pc_ref[0, q0:q0 + QT, 512:768]
    iw = gt_ref[0, 16:24, q0:q0 + QT] * idx_scale
    qlane = lax.broadcasted_iota(jnp.int32, (1, QT), 1)
    lim = ((q0 + qlane) // CHUNK + 1) * CHUNK
    slabs = [(r0, min(DSA_SLAB, nk - r0)) for r0 in range(0, nk, DSA_SLAB)]

    iblk = [_head_pair_block(iq_t[:, hp * 128:(hp + 1) * 128]) for hp in range(IDX_HEADS // 2)]
    for r0, n in slabs:
        ik2 = pc_ref[0, r0:r0 + n, 768:896]
        score = jnp.zeros((n, QT), F32)
        for hp in range(IDX_HEADS // 2):
            r = _dot_nt(ik2, iblk[hp])
            score = score + jnp.maximum(r[:, 0:QT], 0.0) * iw[2 * hp:2 * hp + 1, :]
            score = score + jnp.maximum(r[:, QT:2 * QT], 0.0) * iw[2 * hp + 1:2 * hp + 2, :]
        if r0 + n > q0 + CHUNK:
            krow = r0 + lax.broadcasted_iota(jnp.int32, (n, QT), 0)
            score = jnp.where(krow < lim, score, NEG_INF)
        bits = pltpu.bitcast(score, jnp.int32)
        key_ref[r0:r0 + n, :] = jnp.where(bits >= 0, bits, jnp.int32(INT_MIN) - bits)

    def bit_step(i, carry):
        t, c_t = carry
        cand = t + lax.shift_left(jnp.int32(1), 31 - i)
        acc = jnp.zeros((64, QT), F32)
        for r0 in range(0, nk, 64):
            acc = acc + jnp.where(key_ref[r0:r0 + 64, :] >= cand, 1.0, 0.0)
        cnt = jnp.sum(acc, axis=0, keepdims=True)
        ok = cnt >= n_sel
        return jnp.where(ok, cand, t), jnp.where(ok, cnt, c_t)

    tau, cnt_ge = jnp.full((1, QT), INT_MIN, jnp.int32), jnp.full((1, QT), float(nk), F32)
    if nk > n_sel:
        tau, cnt_ge = lax.fori_loop(0, 32, bit_step, (tau, cnt_ge))
    valid = tau > KEY_NEG_INF
    tau_t = jnp.maximum(tau, KEY_NEG_INF)
    drop = jnp.where(valid, cnt_ge - n_sel, float(nk))
    tri_r = lax.broadcasted_iota(jnp.int32, (DSA_SLAB, DSA_SLAB), 0)
    tri_c = lax.broadcasted_iota(jnp.int32, (DSA_SLAB, DSA_SLAB), 1)
    tri = jnp.where(tri_c >= tri_r, 1.0, 0.0).astype(BF16)
    ablk = [_head_pair_block(aq_t[:, hp * 128:(hp + 1) * 128] * jnp.asarray(att_scale, BF16))
            for hp in range(ATT_HEADS // 2)]
    mx = [jnp.full((64, QT), NEG_INF, F32) for _ in range(ATT_HEADS)]
    later = jnp.zeros((1, QT), F32)
    for r0, n in reversed(slabs):
        key = key_ref[r0:r0 + n, :]
        eq = jnp.where(key == tau_t, 1.0, 0.0)
        rank_slab = _dot(tri[0:n, 0:n], eq.astype(BF16))
        tie_bias = jnp.where(rank_slab > drop - later, 0.0, NEG_INF)
        later = later + rank_slab[0:1, :]
        b = jnp.where(key > tau_t, 0.0, jnp.where(key == tau_t, tie_bias, NEG_INF))
        for hp in range(ATT_HEADS // 2):
            lg = _dot_nt(pc_ref[0, r0:r0 + n, 256 + hp * 128:256 + (hp + 1) * 128], ablk[hp])
            for sub in range(2):
                h = 2 * hp + sub
                l = lg[:, sub * QT:(sub + 1) * QT] + b
                l_ref[h, r0:r0 + n, :] = l
                mx[h] = jnp.maximum(mx[h], jnp.max(l.reshape(n // 64, 64, QT), axis=0))

    outs = []
    for h in range(ATT_HEADS):
        m = jnp.max(mx[h], axis=0, keepdims=True)
        den = jnp.zeros((64, QT), F32)
        for r0, n in slabs:
            e = jnp.exp(l_ref[h, r0:r0 + n, :] - m)
            den = den + jnp.sum(e.reshape(n // 64, 64, QT), axis=0)
            e_ref[r0:r0 + n, :] = e.astype(BF16)
        o_t = _dot(vt_ref[0, h * ATT_DH:(h + 1) * ATT_DH, 0:nk], e_ref[...])
        outs.append(o_t / jnp.sum(den, axis=0, keepdims=True))
    out_t = jnp.concatenate(outs, axis=0)
    out_ref[0] = out_t.T.astype(BF16)


def _dsa_tile(pc, vt, gt, hc, *, nk, n_sel):
    B, S, ncols = pc.shape
    QT = DSA_QT
    tile = nk // QT - 1
    return pl.pallas_call(
        functools.partial(_dsa_body, nk=nk, n_sel=n_sel, q0=nk - QT),
        grid=(B,),
        in_specs=[
            pl.BlockSpec((1, nk, ncols), lambda b: (b, 0, 0)),
            pl.BlockSpec((1, 256, nk), lambda b: (b, 0, 0)),
            pl.BlockSpec((1, 24, nk), lambda b: (b, 0, 0)),
            pl.BlockSpec(memory_space=pl.ANY),
        ],
        out_specs=pl.BlockSpec((1, QT, 256), lambda b: (b, tile, 0)),
        out_shape=jax.ShapeDtypeStruct(hc.shape, hc.dtype),
        input_output_aliases={3: 0},
        scratch_shapes=[
            pltpu.VMEM((nk, QT), jnp.int32),
            pltpu.VMEM((ATT_HEADS, nk, QT), F32),
            pltpu.VMEM((nk, QT), BF16),
        ],
        compiler_params=_cparams(("parallel",)),
        name=f"dsa_{nk}",
    )(pc, vt, gt, hc)


def _dsa(pc, vt, gt):
    B, S, _ = pc.shape
    n_sel = min(TOPK_MAX, S // 4)
    hc = jnp.zeros((B, S, ATT_HEADS * ATT_DH), BF16)
    for nk in range(DSA_QT, S + 1, DSA_QT):
        hc = _dsa_tile(pc, vt, gt, hc, nk=nk, n_sel=n_sel)
    return hc


def _ffn_body(x_ref, ha_ref, hb_ref, hc_ref, woa_ref, wob_ref, woc_ref, gmix_ref,
              gpre_ref, wup_ref, cw_ref, cb_ref, wd_ref, gpost_ref, o_ref, h_ref, act_ref, halo_ref,
              *, tm, dff, tn, rb):
    i = pl.program_id(1)

    @pl.when(i == 0)
    def _():
        halo_ref[...] = jnp.zeros_like(halo_ref)

    def row_block(r, carry):
        r0 = pl.multiple_of(r * rb, rb)
        rows = pl.ds(r0, rb)
        mix = (_dot(ha_ref[0, rows, :], woa_ref[...]) + _dot(hb_ref[0, rows, :], wob_ref[...])
               + _dot(hc_ref[0, rows, :], woc_ref[...]))
        ms_mix = jnp.mean(mix * mix, axis=-1, keepdims=True)
        x1 = x_ref[0, rows, :] + mix * lax.rsqrt(ms_mix + EPS) * gmix_ref[...]
        o_ref[0, rows, :] = x1
        ms = jnp.mean(x1 * x1, axis=-1, keepdims=True)
        h_ref[...] = (x1 * lax.rsqrt(ms + EPS) * gpre_ref[...]).astype(BF16)

        def conv(col0, slot, c0):
            xu = _dot(h_ref[...], wup_ref[:, col0:col0 + tn])
            prev = halo_ref[slot, :, c0:c0 + tn]
            halo_ref[slot, :, c0:c0 + tn] = xu[rb - 8:rb]
            cat = jnp.concatenate([prev, xu], axis=0)
            y = xu * cw_ref[FFN_CONV - 1:FFN_CONV, col0:col0 + tn] + cb_ref[:, col0:col0 + tn]
            for d in range(1, FFN_CONV):
                y = y + pltpu.roll(cat, d, axis=0)[8:8 + rb] * cw_ref[FFN_CONV - 1 - d:FFN_CONV - d, col0:col0 + tn]
            return y

        for c0 in range(0, dff, tn):
            gate = conv(c0, 0, c0)
            up = conv(dff + c0, 1, c0)
            act_ref[:, c0:c0 + tn] = (_gelu_tanh(gate) * up).astype(BF16)
        y = _dot(act_ref[...], wd_ref[...])
        ms_y = jnp.mean(y * y, axis=-1, keepdims=True)
        o_ref[0, rows, :] = o_ref[0, rows, :] + y * lax.rsqrt(ms_y + EPS) * gpost_ref[...]
        return carry

    lax.fori_loop(0, tm // rb, row_block, 0)


def _mix_ffn(x, ha, hb, hc, wo_a, wo_b, wo_c, gmix, gpre, w_up, cw, cb, w_down, gpost, l, *, tm, tn, rb):
    B, S, D = x.shape
    dff = w_down.shape[1]
    row = lambda a: pl.BlockSpec((1, tm, a.shape[2]), lambda b, i: (b, i, 0))
    resident = lambda a: _layer_spec(a, l, pipeline_mode=pl.Buffered(1))
    weights = (wo_a, wo_b, wo_c, gmix, gpre, w_up, cw, cb, w_down, gpost)
    return pl.pallas_call(
        functools.partial(_ffn_body, tm=tm, dff=dff, tn=tn, rb=rb),
        grid=(B, S // tm),
        in_specs=[row(x), row(ha), row(hb), row(hc)] + [resident(w) for w in weights],
        out_specs=row(x),
        out_shape=jax.ShapeDtypeStruct((B, S, D), F32),
        scratch_shapes=[
            pltpu.VMEM((rb, D), BF16),
            pltpu.VMEM((rb, dff), BF16),
            pltpu.VMEM((2, 8, dff), F32),
        ],
        compiler_params=_cparams(("parallel", "arbitrary")),
        name="mix_ffn",
    )(x, ha, hb, hc, *weights)


def _pad_heads(w, n_heads, dh, pad):
    lead = w.shape[:-1]
    w = w.reshape(lead + (n_heads, dh))
    w = jnp.pad(w, [(0, 0)] * len(lead) + [(0, 0), (0, pad - dh)])
    return w.reshape(lead + (n_heads * pad,))


def _block_diag(w):
    depth, nb, bw, _ = w.shape
    eye = jnp.eye(nb, dtype=w.dtype)
    return jnp.einsum("lncd,nm->lncmd", w, eye).reshape(depth, nb * bw, nb * bw)


def _prepare(p):
    H, DH, P = MLSTM_HEADS, MLSTM_DH, HEAD_PAD
    depth, D, _ = p["w_in"].shape
    mw = H * DH
    lw = p["lru_lambda"].shape[-1]
    aw = ATT_HEADS * ATT_DH
    w_in = p["w_in"]
    offs = [0]
    for s in (mw, mw, mw, mw, H, H, lw, lw, aw, aw, aw, IDX_HEADS * IDX_DIM, IDX_DIM, IDX_HEADS):
        offs.append(offs[-1] + s)
    col = lambda k: w_in[:, :, offs[k]:offs[k + 1]]
    mq, mk, mv, mo, mi, mf, lx, lg, aq, ak, av, iq, ik, iw = [col(k) for k in range(14)]

    w_a = jnp.concatenate([_pad_heads(t, H, DH, P) for t in (mq, mk, mv, mo)], axis=2)
    w_l = jnp.concatenate([lx, lg], axis=2)
    w_c = jnp.concatenate([aq, ak, iq, ik, ik], axis=2)
    pad4 = lambda t: jnp.pad(t, ((0, 0), (0, 0), (0, 4)))
    w_t = jnp.concatenate([av, pad4(mi), pad4(mf), pad4(iw)], axis=2)
    row = lambda a: a.reshape(depth, 1, a.shape[-1])
    pad_col = lambda b: jnp.pad(b[:, :, None], ((0, 0), (0, 8 - H), (0, 0)))
    w_out = p["w_out"]
    wo_a = jnp.pad(w_out[:, 0:mw].reshape(depth, H, DH, D), ((0, 0), (0, 0), (0, P - DH), (0, 0)))
    return dict(
        widths=(w_a.shape[2], w_l.shape[2], w_c.shape[2]),
        w_main=jnp.concatenate([w_a, w_l, w_c], axis=2).astype(BF16),
        w_t=jnp.swapaxes(w_t, 1, 2).astype(BF16),
        g_mix_pre=row(p["norm_mix_pre"]), g_mix_post=row(p["norm_mix_post"]),
        g_ffn_pre=row(p["norm_ffn_pre"]), g_ffn_post=row(p["norm_ffn_post"]),
        b_i=pad_col(p["b_igate"]), b_f=pad_col(p["b_fgate"]),
        g_mlstm=_pad_heads(row(p["mlstm_norm"]), H, DH, P),
        lru_cw=p["lru_conv_w"], lru_cb=row(p["lru_conv_b"]),
        lru_wa=_block_diag(p["lru_w_a"]).astype(BF16), lru_ba=row(p["lru_b_a"]),
        lru_wx=_block_diag(p["lru_w_x"]).astype(BF16), lru_bx=row(p["lru_b_x"]),
        lru_lam=row(p["lru_lambda"]),
        wo_a=wo_a.reshape(depth, H * P, D).astype(BF16),
        wo_b=w_out[:, mw:mw + lw].astype(BF16), wo_c=w_out[:, mw + lw:].astype(BF16),
        ffn_up=p["ffn_up"].astype(BF16), ffn_cw=p["ffn_conv_w"], ffn_cb=row(p["ffn_conv_b"]),
        ffn_down=p["ffn_down"].astype(BF16),
    )


def _project(x, w, l):
    na, nl, nc = w["widths"]
    return _proj(x, w["g_mix_pre"], w["w_main"], w["w_t"], l, na=na, nl=nl, nc=nc, tm=min(512, x.shape[1]))


def _layer_parts(x, w, l):
    pa, plx, pc, vt, gt = _project(x, w, l)
    ha, hb = _mixers(pa, gt, w["b_i"], w["b_f"], w["g_mlstm"], plx, w["lru_cw"], w["lru_cb"],
                     w["lru_wa"], w["lru_ba"], w["lru_wx"], w["lru_bx"], w["lru_lam"], l)
    hc = _dsa(pc, vt, gt)
    return ha, hb, hc


def _layer(x, w, l):
    ha, hb, hc = _layer_parts(x, w, l)
    return _mix_ffn(x, ha, hb, hc, w["wo_a"], w["wo_b"], w["wo_c"], w["g_mix_post"], w["g_ffn_pre"],
                    w["ffn_up"], w["ffn_cw"], w["ffn_cb"], w["ffn_down"], w["g_ffn_post"], l,
                    tm=min(1024, x.shape[1]), tn=256, rb=512)


def kernel(x, norm_mix_pre, norm_mix_post, norm_ffn_pre, norm_ffn_post, w_in, b_igate, b_fgate,
           mlstm_norm, lru_conv_w, lru_conv_b, lru_w_a, lru_b_a, lru_w_x, lru_b_x, lru_lambda,
           w_out, ffn_up, ffn_conv_w, ffn_conv_b, ffn_down):
    w = _prepare(dict(norm_mix_pre=norm_mix_pre, norm_mix_post=norm_mix_post, norm_ffn_pre=norm_ffn_pre,
                      norm_ffn_post=norm_ffn_post, w_in=w_in, b_igate=b_igate, b_fgate=b_fgate,
                      mlstm_norm=mlstm_norm, lru_conv_w=lru_conv_w, lru_conv_b=lru_conv_b,
                      lru_w_a=lru_w_a, lru_b_a=lru_b_a, lru_w_x=lru_w_x, lru_b_x=lru_b_x,
                      lru_lambda=lru_lambda, w_out=w_out, ffn_up=ffn_up, ffn_conv_w=ffn_conv_w,
                      ffn_conv_b=ffn_conv_b, ffn_down=ffn_down))
    for l in range(w_in.shape[0]):
        x = _layer(x, w, l)
    return x
```
